```python
import jax, jax.numpy as jnp
from jax import lax
import numpy as np

D_MODEL = 1024
BATCH = 8
SEQ = 4096
DEPTH = 2
DEC_BATCH = 128
DEC_SEQ = 4
PAST_LEN = 16384
PAGE_SIZE = 128

FOX_HEADS = 8
FOX_KV_HEADS = 2
FOX_GROUP = FOX_HEADS // FOX_KV_HEADS
FOX_HEAD_DIM = D_MODEL // 16
FOX_WIDTH = FOX_HEADS * FOX_HEAD_DIM
FORGET_BIAS_INIT = 2.0
CONV_CH = D_MODEL // 2
CONV_WIDTH = 31
MLA_HEADS = 8
MLA_NOPE = 64
MLA_ROPE = 32
MLA_V = 64
MLA_Q_LORA = 384
MLA_KV_LORA = 256
MLA_WIDTH = MLA_HEADS * MLA_V
ROPE_THETA = 10000.0
D_FF = 2752
FFN_CONV_WIDTH = 3
N_BRANCHES = 3
Q_BLOCK = 128
NORM_EPS = 1e-6
NEG_INF = -1e30
SPLIT_SIZES = (FOX_WIDTH, FOX_KV_HEADS * FOX_HEAD_DIM, FOX_KV_HEADS * FOX_HEAD_DIM, FOX_HEADS,
               2 * CONV_CH, MLA_Q_LORA, MLA_KV_LORA, MLA_ROPE, N_BRANCHES * D_MODEL)
IN_COLS = (FOX_WIDTH + 2 * FOX_KV_HEADS * FOX_HEAD_DIM + FOX_HEADS + 2 * CONV_CH
           + MLA_Q_LORA + MLA_KV_LORA + MLA_ROPE + N_BRANCHES * D_MODEL)

kernel_name = 'fox_conformer_mla_hybrid_step'


def _rmsnorm(x, g):
    xf = x.astype(jnp.float32)
    y = xf * lax.rsqrt(jnp.mean(xf * xf, axis=-1, keepdims=True) + NORM_EPS)
    return (y * g.astype(jnp.float32)).astype(x.dtype)


def _layernorm(x, g, b):
    xf = x.astype(jnp.float32)
    mu = jnp.mean(xf, axis=-1, keepdims=True)
    xc = xf - mu
    y = xc * lax.rsqrt(jnp.mean(xc * xc, axis=-1, keepdims=True) + NORM_EPS)
    return (y * g.astype(jnp.float32) + b.astype(jnp.float32)).astype(x.dtype)


def _rope(x, pos):
    half = x.shape[-1] // 2
    inv = ROPE_THETA ** (-jnp.arange(half, dtype=jnp.float32) / half)
    ang = pos.astype(jnp.float32)[:, None] * inv[None, :]
    shape = (1, x.shape[1]) + (1,) * (x.ndim - 3) + (half,)
    cos = jnp.cos(ang).reshape(shape)
    sin = jnp.sin(ang).reshape(shape)
    xf = x.astype(jnp.float32)
    x1, x2 = xf[..., :half], xf[..., half:]
    return jnp.concatenate([x1 * cos - x2 * sin, x2 * cos + x1 * sin], axis=-1).astype(x.dtype)


def _causal_dwconv(x, buf, w, b):
    k = w.shape[0]
    xp = jnp.concatenate([buf, x], axis=1)
    y = lax.conv_general_dilated(xp, w[:, None, :], window_strides=(1,), padding='VALID',
                                 dimension_numbers=('NWC', 'WIO', 'NWC'),
                                 feature_group_count=x.shape[-1])
    return y + b, xp[:, -(k - 1):]


def _sweep_queries(attend, q_args):
    tq = q_args[0].shape[1]
    blk = Q_BLOCK if tq % Q_BLOCK == 0 else tq
    nb = tq // blk
    if nb == 1:
        return attend(*q_args)
    blocks = tuple(jnp.moveaxis(a.reshape((a.shape[0], nb, blk) + a.shape[2:]), 1, 0) for a in q_args)
    out = lax.map(lambda args: attend(*args), blocks)
    out = jnp.moveaxis(out, 0, 1)
    return out.reshape((out.shape[0], tq) + out.shape[3:])


def _fox_attention(q, rq, k, v, rk, q_pos, k_pos):
    scale = FOX_HEAD_DIM ** -0.5
    rk_t = jnp.transpose(rk, (0, 2, 3, 1))

    def attend(qb, rqb, qpb):
        s = jnp.einsum('bqgrd,bkgd->bgrqk', qb, k, preferred_element_type=jnp.float32) * scale
        bias = rk_t[:, :, :, None, :] - jnp.transpose(rqb, (0, 2, 3, 1))[..., None]
        mask = k_pos[None, :] <= qpb[0][:, None]
        s = jnp.where(mask, s + bias, NEG_INF)
        p = jax.nn.softmax(s, axis=-1).astype(v.dtype)
        return jnp.einsum('bgrqk,bkgd->bqgrd', p, v)

    return _sweep_queries(attend, (q, rq, q_pos))


def _mla_attention(q_lat, q_rope, ckv, krope, q_pos, k_pos):
    scale = (MLA_NOPE + MLA_ROPE) ** -0.5

    def attend(ql, qr, qpb):
        s = (jnp.einsum('bqhc,bkc->bhqk', ql, ckv, preferred_element_type=jnp.float32)
             + jnp.einsum('bqhr,bkr->bhqk', qr, krope, preferred_element_type=jnp.float32)) * scale
        mask = k_pos[None, :] <= qpb[0][:, None]
        s = jnp.where(mask, s, NEG_INF)
        p = jax.nn.softmax(s, axis=-1).astype(ckv.dtype)
        return jnp.einsum('bhqk,bkc->bqhc', p, ckv)

    return _sweep_queries(attend, (q_lat, q_rope, q_pos))


def _layer(x, c, lp, past):
    pk, pv, plf, pckv, pkr, conv_buf, ffn_buf = past
    bsz, t, _ = x.shape
    p_len = pk.shape[1]
    mod = (jax.nn.silu(c) @ lp['w_ada'] + lp['b_ada']).reshape(bsz, 6, D_MODEL)
    sh_m, sc_m, gt_m, sh_f, sc_f, gt_f = (mod[:, i, None, :] for i in range(6))

    h = _rmsnorm(x, lp['g_pre_mix']) * (1 + sc_m) + sh_m
    z = h @ lp['w_in']
    split_idx = np.cumsum(SPLIT_SIZES)[:-1].tolist()
    fq, fk, fv, ff, glu, cq, ckv, kr, gl = jnp.split(z, split_idx, axis=-1)
    k_pos = jnp.arange(p_len + t, dtype=jnp.int32)
    q_pos = k_pos[None, p_len:]

    q = fq.reshape(bsz, t, FOX_KV_HEADS, FOX_GROUP, FOX_HEAD_DIM)
    k = fk.reshape(bsz, t, FOX_KV_HEADS, FOX_HEAD_DIM)
    v = fv.reshape(bsz, t, FOX_KV_HEADS, FOX_HEAD_DIM)
    logf = jax.nn.log_sigmoid((ff + lp['b_fox_f']).astype(jnp.float32))
    logf_all = jnp.concatenate([plf.astype(jnp.float32), logf], axis=1)
    r_all = (lax.cumsum(logf_all, axis=1, reverse=True) - logf_all).reshape(bsz, p_len + t, FOX_KV_HEADS, FOX_GROUP)
    o_a = _fox_attention(q, r_all[:, p_len:], jnp.concatenate([pk, k], axis=1),
                         jnp.concatenate([pv, v], axis=1), r_all, q_pos, k_pos)
    br_a = o_a.reshape(bsz, t, FOX_WIDTH) @ lp['w_fox_out']

    ga, gb = jnp.split(glu, 2, axis=-1)
    u = ga * jax.nn.sigmoid(gb)
    uc, new_conv = _causal_dwconv(u, conv_buf, lp['w_dw'], lp['b_dw'])
    br_b = jax.nn.silu(_layernorm(uc, lp['ln_g'], lp['ln_b'])) @ lp['w_conv_out']

    pos = k_pos[p_len:]
    qc = (_rmsnorm(cq, lp['g_cq']) @ lp['w_uq']).reshape(bsz, t, MLA_HEADS, MLA_NOPE + MLA_ROPE)
    q_nope = qc[..., :MLA_NOPE]
    q_rope = _rope(qc[..., MLA_NOPE:], pos)
    q_lat = jnp.einsum('bqhn,hcn->bqhc', q_nope, lp['w_uk'])
    ckv = _rmsnorm(ckv, lp['g_ckv'])
    kr = _rope(kr, pos)
    o_lat = _mla_attention(q_lat, q_rope, jnp.concatenate([pckv, ckv], axis=1),
                           jnp.concatenate([pkr, kr], axis=1), q_pos, k_pos)
    br_c = jnp.einsum('bqhc,hcd->bqhd', o_lat, lp['w_uv']).reshape(bsz, t, MLA_WIDTH) @ lp['w_mla_out']

    gates = jax.nn.sigmoid(gl).reshape(bsz, t, N_BRANCHES, D_MODEL)
    merged = gates[:, :, 0] * br_a + gates[:, :, 1] * br_b + gates[:, :, 2] * br_c
    x = x + gt_m * _rmsnorm(merged @ lp['w_out'], lp['g_post_mix'])

    h = _rmsnorm(x, lp['g_pre_ffn']) * (1 + sc_f) + sh_f
    gu, new_ffn = _causal_dwconv(h @ lp['w_gate'], ffn_buf, lp['w_ffn_dw'], lp['b_ffn_dw'])
    y = (jax.nn.silu(gu) * (h @ lp['w_val'])) @ lp['w_down']
    x = x + gt_f * _rmsnorm(y, lp['g_post_ffn'])
    return x, (k, v, logf.astype(x.dtype), ckv, kr, new_conv, new_ffn)


def _gather_pages(pool, l, page_table):
    pages = pool[l, page_table]
    return pages.reshape((pages.shape[0], pages.shape[1] * pages.shape[2]) + pages.shape[3:])


def setup_inputs(seed: int = 0) -> dict:
    key = jax.random.key(seed)
    kit = iter(list(jax.random.split(key, 64)))

    def nrm(shape, scale=1.0):
        return scale * jax.random.normal(next(kit), shape, jnp.float32)

    L, D = DEPTH, D_MODEL
    n_pages = PAST_LEN // PAGE_SIZE
    n_used = DEC_BATCH * n_pages
    n_phys = n_used + n_used // 4
    page_table = jax.random.permutation(next(kit), n_phys)[:n_used].reshape(DEC_BATCH, n_pages).astype(jnp.int32)
    return {
        'x_prompt': nrm((BATCH, SEQ, D)),
        'x_sample': nrm((DEC_BATCH, DEC_SEQ, D)),
        'c_prompt': nrm((BATCH, D)),
        'c_sample': nrm((DEC_BATCH, D)),
        'cache_fox_k': nrm((L, n_phys, PAGE_SIZE, FOX_KV_HEADS, FOX_HEAD_DIM)),
        'cache_fox_v': nrm((L, n_phys, PAGE_SIZE, FOX_KV_HEADS, FOX_HEAD_DIM)),
        'cache_fox_logf': jax.nn.log_sigmoid(FORGET_BIAS_INIT + nrm((L, n_phys, PAGE_SIZE, FOX_HEADS))),
        'cache_mla_ckv': nrm((L, n_phys, PAGE_SIZE, MLA_KV_LORA)),
        'cache_mla_krope': nrm((L, n_phys, PAGE_SIZE, MLA_ROPE)),
        'state_conv': nrm((L, DEC_BATCH, CONV_WIDTH - 1, CONV_CH), 0.5),
        'state_ffn_conv': nrm((L, DEC_BATCH, FFN_CONV_WIDTH - 1, D_FF)),
        'page_table': page_table,
        'w_ada': nrm((L, D, 6 * D), 0.5 * D ** -0.5),
        'b_ada': nrm((L, 6 * D), 0.02),
        'g_pre_mix': 1.0 + nrm((L, D), 0.05),
        'g_post_mix': 1.0 + nrm((L, D), 0.05),
        'g_pre_ffn': 1.0 + nrm((L, D), 0.05),
        'g_post_ffn': 1.0 + nrm((L, D), 0.05),
        'w_in': nrm((L, D, IN_COLS), D ** -0.5),
        'b_fox_f': FORGET_BIAS_INIT + nrm((L, FOX_HEADS), 0.1),
        'w_fox_out': nrm((L, FOX_WIDTH, D), FOX_WIDTH ** -0.5),
        'w_dw': nrm((L, CONV_WIDTH, CONV_CH), CONV_WIDTH ** -0.5),
        'b_dw': nrm((L, CONV_CH), 0.02),
        'ln_g': 1.0 + nrm((L, CONV_CH), 0.05),
        'ln_b': nrm((L, CONV_CH), 0.02),
        'w_conv_out': nrm((L, CONV_CH, D), CONV_CH ** -0.5),
        'g_cq': 1.0 + nrm((L, MLA_Q_LORA), 0.05),
        'w_uq': nrm((L, MLA_Q_LORA, MLA_HEADS * (MLA_NOPE + MLA_ROPE)), MLA_Q_LORA ** -0.5),
        'g_ckv': 1.0 + nrm((L, MLA_KV_LORA), 0.05),
        'w_uk': nrm((L, MLA_HEADS, MLA_KV_LORA, MLA_NOPE), MLA_KV_LORA ** -0.5),
        'w_uv': nrm((L, MLA_HEADS, MLA_KV_LORA, MLA_V), MLA_KV_LORA ** -0.5),
        'w_mla_out': nrm((L, MLA_WIDTH, D), MLA_WIDTH ** -0.5),
        'w_out': nrm((L, D, D), D ** -0.5),
        'w_gate': nrm((L, D, D_FF), D ** -0.5),
        'w_val': nrm((L, D, D_FF), D ** -0.5),
        'w_ffn_dw': nrm((L, FFN_CONV_WIDTH, D_FF), FFN_CONV_WIDTH ** -0.5),
        'b_ffn_dw': nrm((L, D_FF), 0.02),
        'w_down': nrm((L, D_FF, D), D_FF ** -0.5),
    }


def reference(x_prompt, x_sample, c_prompt, c_sample, cache_fox_k, cache_fox_v, cache_fox_logf,
              cache_mla_ckv, cache_mla_krope, state_conv, state_ffn_conv, page_table,
              w_ada, b_ada, g_pre_mix, g_post_mix, g_pre_ffn, g_post_ffn, w_in, b_fox_f, w_fox_out,
              w_dw, b_dw, ln_g, ln_b, w_conv_out, g_cq, w_uq, g_ckv, w_uk, w_uv, w_mla_out, w_out,
              w_gate, w_val, w_ffn_dw, b_ffn_dw, w_down):
    xp, xs = x_prompt, x_sample
    bp = x_prompt.shape[0]
    dt = x_prompt.dtype
    past_p = (jnp.zeros((bp, 0, FOX_KV_HEADS, FOX_HEAD_DIM), dt),
              jnp.zeros((bp, 0, FOX_KV_HEADS, FOX_HEAD_DIM), dt),
              jnp.zeros((bp, 0, FOX_HEADS), dt),
              jnp.zeros((bp, 0, MLA_KV_LORA), dt),
              jnp.zeros((bp, 0, MLA_ROPE), dt),
              jnp.zeros((bp, CONV_WIDTH - 1, CONV_CH), dt),
              jnp.zeros((bp, FFN_CONV_WIDTH - 1, D_FF), dt))
    new_p, new_s = [], []
    for l in range(DEPTH):
        lp = {'w_ada': w_ada[l], 'b_ada': b_ada[l], 'g_pre_mix': g_pre_mix[l], 'g_post_mix': g_post_mix[l],
              'g_pre_ffn': g_pre_ffn[l], 'g_post_ffn': g_post_ffn[l], 'w_in': w_in[l], 'b_fox_f': b_fox_f[l],
              'w_fox_out': w_fox_out[l], 'w_dw': w_dw[l], 'b_dw': b_dw[l], 'ln_g': ln_g[l], 'ln_b': ln_b[l],
              'w_conv_out': w_conv_out[l], 'g_cq': g_cq[l], 'w_uq': w_uq[l], 'g_ckv': g_ckv[l],
              'w_uk': w_uk[l], 'w_uv': w_uv[l], 'w_mla_out': w_mla_out[l], 'w_out': w_out[l],
              'w_gate': w_gate[l], 'w_val': w_val[l], 'w_ffn_dw': w_ffn_dw[l], 'b_ffn_dw': b_ffn_dw[l],
              'w_down': w_down[l]}
        xp, st_p = _layer(xp, c_prompt, lp, past_p)
        new_p.append(st_p)
        past_s = (_gather_pages(cache_fox_k, l, page_table),
                  _gather_pages(cache_fox_v, l, page_table),
                  _gather_pages(cache_fox_logf, l, page_table),
                  _gather_pages(cache_mla_ckv, l, page_table),
                  _gather_pages(cache_mla_krope, l, page_table),
                  state_conv[l], state_ffn_conv[l])
        xs, st_s = _layer(xs, c_sample, lp, past_s)
        new_s.append(st_s)
    fox_k_p = jnp.stack([s[0] for s in new_p])
    fox_v_p = jnp.stack([s[1] for s in new_p])
    fox_logf_p = jnp.stack([s[2] for s in new_p])
    mla_ckv_p = jnp.stack([s[3] for s in new_p])
    mla_krope_p = jnp.stack([s[4] for s in new_p])
    conv_state_p = jnp.stack([s[5] for s in new_p])
    ffn_state_p = jnp.stack([s[6] for s in new_p])
    fox_k_s = jnp.stack([s[0] for s in new_s])
    fox_v_s = jnp.stack([s[1] for s in new_s])
    fox_logf_s = jnp.stack([s[2] for s in new_s])
    mla_ckv_s = jnp.stack([s[3] for s in new_s])
    mla_krope_s = jnp.stack([s[4] for s in new_s])
    conv_state_s = jnp.stack([s[5] for s in new_s])
    ffn_state_s = jnp.stack([s[6] for s in new_s])
    return (xp, xs, fox_k_p, fox_v_p, fox_logf_p, mla_ckv_p, mla_krope_p, conv_state_p, ffn_state_p,
            fox_k_s, fox_v_s, fox_logf_s, mla_ckv_s, mla_krope_s, conv_state_s, ffn_state_s)
```

```python
import functools

import numpy as np
import jax
import jax.numpy as jnp
from jax import lax
from jax.experimental import pallas as pl
from jax.experimental.pallas import tpu as pltpu

F32 = jnp.float32
BF16 = jnp.bfloat16
NORM_EPS = 1e-6
NEG_INF = -1e30
ROPE_THETA = 10000.0
LANES = 128
VMEM_LIMIT = 56 * 1024 * 1024
ROW_TILE = 256
ATT_TILE = 256
CHUNK_PAGES = 16


def _cparams(*sem):
    return pltpu.CompilerParams(dimension_semantics=sem, vmem_limit_bytes=VMEM_LIMIT)


def _const_spec(shape):
    nd = len(shape)
    return pl.BlockSpec(shape, lambda *_: (0,) * nd)


def _dot(a, b):
    return jnp.dot(a, b, preferred_element_type=F32)


def _dot_nt(a, b):
    return lax.dot_general(a, b, (((1,), (1,)), ((), ())), preferred_element_type=F32)


def _split3(x):
    hi = x.astype(BF16)
    r = x - hi.astype(F32)
    mid = r.astype(BF16)
    lo = (r - mid.astype(F32)).astype(BF16)
    return hi, mid, lo


def _sigmoid(x):
    return 1.0 / (1.0 + jnp.exp(-x))


def _silu(x):
    return x * _sigmoid(x)


def _log_sigmoid(x):
    return jnp.minimum(x, 0.0) - jnp.log1p(jnp.exp(-jnp.abs(x)))


def _rms(x, g):
    return x * lax.rsqrt(jnp.mean(x * x, axis=-1, keepdims=True) + NORM_EPS) * g


def _group_rows(a2d, n_groups, t, tm):
    d = a2d.shape[-1]
    if t % tm == 0:
        per = t // tm
        return a2d.reshape(n_groups, 1, d), pl.BlockSpec((None, 1, d), lambda i: (i // per, 0, 0))
    assert tm % t == 0
    return jnp.repeat(a2d, t, axis=0), pl.BlockSpec((tm, d), lambda i: (i, 0))


def _pos_rows(tab, n_groups, t, tm):
    w = tab.shape[-1]
    if t % tm == 0:
        per = t // tm
        return tab, pl.BlockSpec((tm, w), lambda i: (i % per, 0))
    assert tm % t == 0
    return jnp.tile(tab, (n_groups, 1)), pl.BlockSpec((tm, w), lambda i: (i, 0))


def _ada_kernel(c_ref, w_ref, b_ref, o_ref):
    o_ref[...] = _dot(_silu(c_ref[...]).astype(BF16), w_ref[...]) + b_ref[...]


def _ada(c_all, w, b):
    m, d = c_all.shape
    n = w.shape[1]
    tn = 1024
    return pl.pallas_call(
        _ada_kernel,
        grid=(n // tn,),
        in_specs=[_const_spec((m, d)), pl.BlockSpec((d, tn), lambda j: (0, j)), pl.BlockSpec((1, tn), lambda j: (0, j))],
        out_specs=pl.BlockSpec((m, tn), lambda j: (0, j)),
        out_shape=jax.ShapeDtypeStruct((m, n), F32),
        compiler_params=_cparams("arbitrary"),
        name="ada",
    )(c_all, w, b)


def _mm_kernel(x_ref, w_ref, o_ref):
    o_ref[...] = _dot(x_ref[...].astype(BF16), w_ref[...]).astype(o_ref.dtype)


def _mm(x, w, out_dtype, name):
    m, k = x.shape
    n = w.shape[1]
    tm = min(ROW_TILE, m)
    tn = min(1024, n)
    return pl.pallas_call(
        _mm_kernel,
        grid=(m // tm, n // tn),
        in_specs=[pl.BlockSpec((tm, k), lambda i, j: (i, 0)), pl.BlockSpec((k, tn), lambda i, j: (0, j))],
        out_specs=pl.BlockSpec((tm, tn), lambda i, j: (i, j)),
        out_shape=jax.ShapeDtypeStruct((m, n), out_dtype),
        compiler_params=_cparams("arbitrary", "arbitrary"),
        name=name,
    )(x, w)


def _inproj_kernel(x_ref, g_ref, sc_ref, sh_ref, cos_ref, sin_ref, bf_ref, gcq_ref, gckv_ref,
                   wq, wk, wv, wf, wga, wgb, wcq, wckv, wkr, wkrp, wgl,
                   oq, ok, ov, olf, ou, ocq, ockv, okr, og, *, q_scale, n_fox_heads):
    h = (_rms(x_ref[...], g_ref[...]) * (1.0 + sc_ref[...]) + sh_ref[...]).astype(BF16)
    oq[...] = _dot(h, wq[...]) * q_scale
    ok[...] = _dot(h, wk[...])
    ov[...] = _dot(h, wv[...])
    lf = _log_sigmoid(_dot(h, wf[...]) + bf_ref[...])
    lane = lax.broadcasted_iota(jnp.int32, lf.shape, 1)
    olf[...] = jnp.where(lane < n_fox_heads, lf, 0.0)
    ou[...] = _dot(h, wga[...]) * _sigmoid(_dot(h, wgb[...]))
    ocq[...] = _rms(_dot(h, wcq[...]), gcq_ref[...]).astype(BF16)
    ockv[...] = _rms(_dot(h, wckv[...]), gckv_ref[...])
    okr[...] = _dot(h, wkr[...]) * cos_ref[...] + _dot(h, wkrp[...]) * sin_ref[...]
    og[...] = _sigmoid(_dot(h, wgl[...]))


def _inproj(x, n_groups, t, g_pre, sc, sh, cos_kr, sin_kr, bf, gcq, gckv, ws, q_scale, n_fox_heads):
    m, d = x.shape
    tm = min(ROW_TILE, m)
    sc_a, sc_s = _group_rows(sc, n_groups, t, tm)
    sh_a, sh_s = _group_rows(sh, n_groups, t, tm)
    cos_a, cos_s = _pos_rows(cos_kr, n_groups, t, tm)
    sin_a, sin_s = _pos_rows(sin_kr, n_groups, t, tm)
    row = lambda w, dt: (pl.BlockSpec((tm, w), lambda i: (i, 0)), jax.ShapeDtypeStruct((m, w), dt))
    outs = [row(ws["wq"].shape[1], F32), row(ws["wk"].shape[1], F32), row(ws["wv"].shape[1], F32), row(LANES, F32),
            row(ws["wga"].shape[1], F32), row(ws["wcq"].shape[1], BF16), row(ws["wckv"].shape[1], F32),
            row(LANES, F32), row(ws["wgl"].shape[1], F32)]
    wnames = ["wq", "wk", "wv", "wf", "wga", "wgb", "wcq", "wckv", "wkr", "wkrp", "wgl"]
    vecs = [g_pre, bf, gcq, gckv]
    in_specs = ([pl.BlockSpec((tm, d), lambda i: (i, 0)), _const_spec(g_pre.shape), sc_s, sh_s, cos_s, sin_s]
                + [_const_spec(v.shape) for v in vecs[1:]] + [_const_spec(ws[n].shape) for n in wnames])
    return pl.pallas_call(
        functools.partial(_inproj_kernel, q_scale=q_scale, n_fox_heads=n_fox_heads),
        grid=(m // tm,),
        in_specs=in_specs,
        out_specs=[o[0] for o in outs],
        out_shape=[o[1] for o in outs],
        compiler_params=_cparams("arbitrary"),
        name="inproj",
    )(x, g_pre, sc_a, sh_a, cos_a, sin_a, bf, gcq, gckv, *[ws[n] for n in wnames])


def _fox_prep_kernel(q_ref, k_ref, v_ref, lf_ref, eq, ech, ecm, ecl, qconst, ek, ekh, ekm, ekl, kconst, ev,
                     oq, ok, ov, carry, *, tiles_per_seq):
    tm = q_ref.shape[0]

    @pl.when(pl.program_id(0) % tiles_per_seq == 0)
    def _():
        carry[...] = jnp.zeros_like(carry)

    r = lax.broadcasted_iota(jnp.int32, (tm, tm), 0)
    c = lax.broadcasted_iota(jnp.int32, (tm, tm), 1)
    tri = (c <= r).astype(BF16)
    l_hi, l_mid, l_lo = _split3(lf_ref[...])
    cum = _dot(tri, l_hi) + _dot(tri, l_mid) + _dot(tri, l_lo) + carry[0:1, :]
    carry[...] = jnp.broadcast_to(cum[tm - 1:tm, :], carry.shape)
    c_hi, c_mid, c_lo = _split3(cum)
    qp = (_dot(q_ref[...].astype(BF16), eq[...]) + _dot(c_hi, ech[...]) + _dot(c_mid, ecm[...])
          + _dot(c_lo, ecl[...]) + qconst[...])
    kp = (_dot(k_ref[...].astype(BF16), ek[...]) + _dot(c_hi, ekh[...]) + _dot(c_mid, ekm[...])
          + _dot(c_lo, ekl[...]) + kconst[...])
    oq[...] = qp.astype(BF16)
    ok[...] = kp.astype(BF16)
    ov[...] = _dot(v_ref[...].astype(BF16), ev[...]).astype(BF16)


def _fox_prep_consts(n_heads, n_kv, dh):
    grp = n_heads // n_kv
    eq = np.zeros((n_heads * dh, n_heads * LANES), np.float32)
    ec = np.zeros((3, LANES, n_heads * LANES), np.float32)
    qconst = np.zeros((1, n_heads * LANES), np.float32)
    ek = np.zeros((n_kv * dh, n_kv * LANES), np.float32)
    ekc = np.zeros((3, LANES, n_kv * LANES), np.float32)
    kconst = np.zeros((1, n_kv * LANES), np.float32)
    ev = np.zeros((n_kv * dh, n_kv * 2 * LANES), np.float32)
    for h in range(n_heads):
        kv, g = h // grp, h % grp
        eq[h * dh + np.arange(dh), h * LANES + np.arange(dh)] = 1.0
        for j in range(3):
            ec[j, h, h * LANES + dh + j] = 1.0
            qconst[0, h * LANES + dh + 3 + 3 * g + j] = 1.0
            ekc[j, h, kv * LANES + dh + 3 + 3 * g + j] = -1.0
    for kv in range(n_kv):
        ek[kv * dh + np.arange(dh), kv * LANES + np.arange(dh)] = 1.0
        kconst[0, kv * LANES + dh + np.arange(3)] = 1.0
        for side in range(2):
            ev[kv * dh + np.arange(dh), (kv * 2 + side) * LANES + side * dh + np.arange(dh)] = 1.0
    b = lambda a: jnp.asarray(a, BF16)
    return (b(eq), b(ec[0]), b(ec[1]), b(ec[2]), jnp.asarray(qconst), b(ek), b(ekc[0]), b(ekc[1]), b(ekc[2]),
            jnp.asarray(kconst), b(ev))


def _fox_prep(oq, ok, ov, olf, t, n_heads, n_kv, dh):
    m = oq.shape[0]
    tm = min(ROW_TILE, t)
    consts = _fox_prep_consts(n_heads, n_kv, dh)
    row = lambda a: pl.BlockSpec((tm, a.shape[1]), lambda i: (i, 0))
    widths = (n_heads * LANES, n_kv * LANES, n_kv * 2 * LANES)
    return pl.pallas_call(
        functools.partial(_fox_prep_kernel, tiles_per_seq=t // tm),
        grid=(m // tm,),
        in_specs=[row(oq), row(ok), row(ov), row(olf)] + [_const_spec(c.shape) for c in consts],
        out_specs=[pl.BlockSpec((tm, w), lambda i: (i, 0)) for w in widths],
        out_shape=[jax.ShapeDtypeStruct((m, w), BF16) for w in widths],
        scratch_shapes=[pltpu.VMEM((8, LANES), F32)],
        compiler_params=_cparams("arbitrary"),
        name="fox_prep",
    )(oq, ok, ov, olf, *consts)


def _mla_q_kernel(cq_ref, cos_ref, sin_ref, wa, wb, oq, *, n_heads):
    cq = cq_ref[...]
    qa = _dot(cq, wa[...])
    qb = _dot(cq, wb[...])
    cos = cos_ref[...]
    sin = sin_ref[...]
    for h in range(n_heads):
        sl = slice(h * LANES, (h + 1) * LANES)
        oq[:, sl] = (qa[:, sl] * cos + qb[:, sl] * sin).astype(BF16)


def _mla_q(ocq, n_groups, t, cos_q, sin_q, wa, wb, n_heads):
    m, k = ocq.shape
    tm = min(ROW_TILE, m)
    cos_a, cos_s = _pos_rows(cos_q, n_groups, t, tm)
    sin_a, sin_s = _pos_rows(sin_q, n_groups, t, tm)
    n = wa.shape[1]
    return pl.pallas_call(
        functools.partial(_mla_q_kernel, n_heads=n_heads),
        grid=(m // tm,),
        in_specs=[pl.BlockSpec((tm, k), lambda i: (i, 0)), cos_s, sin_s, _const_spec(wa.shape), _const_spec(wb.shape)],
        out_specs=pl.BlockSpec((tm, n), lambda i: (i, 0)),
        out_shape=jax.ShapeDtypeStruct((m, n), BF16),
        compiler_params=_cparams("arbitrary"),
        name="mla_q",
    )(ocq, cos_a, sin_a, wa, wb)


def _mla_kv_kernel(ckv_ref, kr_ref, wk, ekr, wv, ok, ov):
    ckv = ckv_ref[...].astype(BF16)
    ok[...] = (_dot(ckv, wk[...]) + _dot(kr_ref[...].astype(BF16), ekr[...])).astype(BF16)
    ov[...] = _dot(ckv, wv[...]).astype(BF16)


def _mla_kv(ockv, okr, wk, ekr, wv):
    m = ockv.shape[0]
    tm = min(ROW_TILE, m)
    row = lambda a: pl.BlockSpec((tm, a.shape[1]), lambda i: (i, 0))
    return pl.pallas_call(
        _mla_kv_kernel,
        grid=(m // tm,),
        in_specs=[row(ockv), row(okr), _const_spec(wk.shape), _const_spec(ekr.shape), _const_spec(wv.shape)],
        out_specs=[pl.BlockSpec((tm, wk.shape[1]), lambda i: (i, 0)), pl.BlockSpec((tm, wv.shape[1]), lambda i: (i, 0))],
        out_shape=[jax.ShapeDtypeStruct((m, wk.shape[1]), BF16), jax.ShapeDtypeStruct((m, wv.shape[1]), BF16)],
        compiler_params=_cparams("arbitrary"),
        name="mla_kv",
    )(ockv, okr, wk, ekr, wv)


def _flash_kernel(q_ref, k_ref, v_ref, o_ref, m_sc, l_sc, acc_sc, *, k_slot, v_slot, blk):
    qi = pl.program_id(1)
    n_heads = len(k_slot)
    row = lax.broadcasted_iota(jnp.int32, (blk, blk), 0)
    col = lax.broadcasted_iota(jnp.int32, (blk, blk), 1)
    causal = col <= row

    def step(q, ks, vs, side, ki, masked):
        k = k_ref[pl.ds(ki * blk, blk), ks * LANES:(ks + 1) * LANES]
        v = v_ref[pl.ds(ki * blk, blk), vs * LANES:(vs + 1) * LANES]
        s = _dot_nt(q, k)
        if masked:
            s = jnp.where(causal, s, NEG_INF)
        m_prev = m_sc[side]
        m_new = jnp.maximum(m_prev, jnp.max(s, axis=-1, keepdims=True))
        alpha = jnp.exp(m_prev - m_new)
        p = jnp.exp(s - m_new)
        l_sc[side] = alpha * l_sc[side] + jnp.sum(p, axis=-1, keepdims=True)
        acc_sc[side] = alpha * acc_sc[side] + _dot(p.astype(BF16), v)
        m_sc[side] = m_new

    for h in range(n_heads):
        side = h % 2
        q = q_ref[:, h * LANES:(h + 1) * LANES]
        m_sc[side] = jnp.full((blk, 1), NEG_INF, F32)
        l_sc[side] = jnp.zeros((blk, 1), F32)
        acc_sc[side] = jnp.zeros((blk, LANES), F32)

        def body(ki, carry, q=q, h=h, side=side):
            step(q, k_slot[h], v_slot[h], side, ki, False)
            return carry

        lax.fori_loop(0, qi, body, 0)
        step(q, k_slot[h], v_slot[h], side, qi, True)
        if side == 1:
            out = acc_sc[0] / l_sc[0] + acc_sc[1] / l_sc[1]
            o_ref[:, (h // 2) * LANES:(h // 2 + 1) * LANES] = out.astype(o_ref.dtype)


def _flash(q, k, v, n_seq, t, k_slot, v_slot):
    m, qw = q.shape
    blk = min(ATT_TILE, t)
    nq = t // blk
    ow = (len(k_slot) // 2) * LANES
    return pl.pallas_call(
        functools.partial(_flash_kernel, k_slot=tuple(k_slot), v_slot=tuple(v_slot), blk=blk),
        grid=(n_seq, nq),
        in_specs=[pl.BlockSpec((blk, qw), lambda b, i: (b * nq + i, 0)),
                  pl.BlockSpec((t, k.shape[1]), lambda b, i: (b, 0)),
                  pl.BlockSpec((t, v.shape[1]), lambda b, i: (b, 0))],
        out_specs=pl.BlockSpec((blk, ow), lambda b, i: (b * nq + i, 0)),
        out_shape=jax.ShapeDtypeStruct((m, ow), BF16),
        scratch_shapes=[pltpu.VMEM((2, blk, 1), F32), pltpu.VMEM((2, blk, 1), F32), pltpu.VMEM((2, blk, LANES), F32)],
        compiler_params=_cparams("arbitrary", "arbitrary"),
        name="flash",
    )(q, k, v)


def _dwconv_kernel(u_ref, st_ref, w_ref, b_ref, o_ref, xp, *, n_taps, hist):
    tm = u_ref.shape[0]
    t = pl.program_id(1)

    @pl.when(t == 0)
    def _():
        xp[0:hist, :] = st_ref[...]

    @pl.when(t > 0)
    def _():
        xp[0:hist, :] = xp[tm:tm + hist, :]

    xp[hist:hist + tm, :] = u_ref[...]
    off = hist - (n_taps - 1)
    for c0 in range(0, u_ref.shape[1], LANES):
        cs = slice(c0, c0 + LANES)
        acc = jnp.broadcast_to(b_ref[:, cs], (tm, LANES))
        for j in range(n_taps):
            acc = acc + w_ref[j:j + 1, cs] * xp[off + j:off + j + tm, cs]
        o_ref[:, cs] = acc


def _dwconv(u, state, w, b, n_seq, t):
    m, ch = u.shape
    n_taps = w.shape[0]
    hist = 32
    assert n_taps - 1 <= hist
    tm = min(ROW_TILE, t)
    st = jnp.pad(state, ((0, 0), (hist - (n_taps - 1), 0), (0, 0)))
    wp = jnp.pad(w, ((0, hist - n_taps), (0, 0)))
    out = pl.pallas_call(
        functools.partial(_dwconv_kernel, n_taps=n_taps, hist=hist),
        grid=(n_seq, t // tm),
        in_specs=[pl.BlockSpec((None, tm, ch), lambda s, i: (s, i, 0)),
                  pl.BlockSpec((None, hist, ch), lambda s, i: (s, 0, 0)),
                  _const_spec(wp.shape), _const_spec(b.shape)],
        out_specs=pl.BlockSpec((None, tm, ch), lambda s, i: (s, i, 0)),
        out_shape=jax.ShapeDtypeStruct((n_seq, t, ch), F32),
        scratch_shapes=[pltpu.VMEM((tm + hist, ch), F32)],
        compiler_params=_cparams("arbitrary", "arbitrary"),
        name="dwconv",
    )(u.reshape(n_seq, t, ch), st, wp, b)
    return out.reshape(m, ch)


def _merge_kernel(oa_ref, uc_ref, oc_ref, g_ref, x_ref, gt_ref, lng, lnb, gpost, wfo, wco, wmo, wout, o_ref):
    d = x_ref.shape[1]
    uc = uc_ref[...]
    xc = uc - jnp.mean(uc, axis=-1, keepdims=True)
    y = xc * lax.rsqrt(jnp.mean(xc * xc, axis=-1, keepdims=True) + NORM_EPS) * lng[...] + lnb[...]
    br_a = _dot(oa_ref[...], wfo[...])
    br_b = _dot(_silu(y).astype(BF16), wco[...])
    br_c = _dot(oc_ref[...], wmo[...])
    merged = g_ref[:, 0:d] * br_a + g_ref[:, d:2 * d] * br_b + g_ref[:, 2 * d:3 * d] * br_c
    z = _dot(merged.astype(BF16), wout[...])
    o_ref[...] = x_ref[...] + gt_ref[...] * _rms(z, gpost[...])


def _merge(oa, uc, oc, gates, x, gt, n_groups, t, lng, lnb, gpost, wfo, wco, wmo, wout):
    m, d = x.shape
    tm = min(ROW_TILE, m)
    gt_a, gt_s = _group_rows(gt, n_groups, t, tm)
    row = lambda a: pl.BlockSpec((tm, a.shape[1]), lambda i: (i, 0))
    consts = [lng, lnb, gpost, wfo, wco, wmo, wout]
    return pl.pallas_call(
        _merge_kernel,
        grid=(m // tm,),
        in_specs=[row(oa), row(uc), row(oc), row(gates), row(x), gt_s] + [_const_spec(c.shape) for c in consts],
        out_specs=pl.BlockSpec((tm, d), lambda i: (i, 0)),
        out_shape=jax.ShapeDtypeStruct((m, d), F32),
        compiler_params=_cparams("arbitrary"),
        name="merge",
    )(oa, uc, oc, gates, x, gt_a, *consts)


def _ffn_kernel(x_ref, g_ref, sc_ref, sh_ref, gt_ref, s0_ref, s1_ref, wg, wv, wdw, bdw, wd, gpost,
                o_ref, a_ref, carry, *, seq_len, col_chunk, emit_all):
    tm, _ = x_ref.shape
    ffp = wg.shape[1]
    x = x_ref[...]
    h = (_rms(x, g_ref[...]) * (1.0 + sc_ref[...]) + sh_ref[...]).astype(BF16)

    @pl.when(pl.program_id(0) == 0)
    def _():
        carry[...] = jnp.zeros_like(carry)

    r = lax.broadcasted_iota(jnp.int32, (tm, 1), 0)
    pos = (pl.program_id(0) * tm + r) % seq_len
    y = jnp.zeros(o_ref.shape, F32)
    for c0 in range(0, ffp, col_chunk):
        cs = slice(c0, c0 + col_chunk)
        a = _dot(h, wg[:, cs])
        val = _dot(h, wv[:, cs])
        prev = carry[:, cs]
        a1 = pltpu.roll(a, 1, 0)
        a1 = jnp.where(r == 0, prev[7:8, :], a1)
        a2 = pltpu.roll(a, 2, 0)
        a2 = jnp.where(r == 0, prev[6:7, :], jnp.where(r == 1, prev[7:8, :], a2))
        s0 = s0_ref[:, cs]
        s1 = s1_ref[:, cs]
        a1 = jnp.where(pos >= 1, a1, s1)
        a2 = jnp.where(pos >= 2, a2, jnp.where(pos == 0, s0, s1))
        gu = wdw[0:1, cs] * a2 + wdw[1:2, cs] * a1 + wdw[2:3, cs] * a + bdw[:, cs]
        carry[:, cs] = a[tm - 8:tm, :]
        if emit_all:
            a_ref[:, cs] = a
        else:
            a_ref[:, cs] = a[tm - 8:tm, :]
        y = y + _dot((_silu(gu) * val).astype(BF16), wd[cs, :])
    o_ref[...] = x + gt_ref[...] * _rms(y, gpost[...])


def _ffn(x, n_groups, t, g_pre, sc, sh, gt, s0, s1, wg, wv, wdw, bdw, wd, gpost, emit_all):
    m, d = x.shape
    ffp = wg.shape[1]
    tm = min(ROW_TILE, m)
    n_tiles = m // tm
    sc_a, sc_s = _group_rows(sc, n_groups, t, tm)
    sh_a, sh_s = _group_rows(sh, n_groups, t, tm)
    gt_a, gt_s = _group_rows(gt, n_groups, t, tm)
    s0_a, s0_s = _group_rows(s0, n_groups, t, tm)
    s1_a, s1_s = _group_rows(s1, n_groups, t, tm)
    consts = [wg, wv, wdw, bdw, wd, gpost]
    if emit_all:
        a_spec, a_shape = pl.BlockSpec((tm, ffp), lambda i: (i, 0)), jax.ShapeDtypeStruct((m, ffp), F32)
    else:
        a_spec, a_shape = pl.BlockSpec((None, 8, ffp), lambda i: (i, 0, 0)), jax.ShapeDtypeStruct((n_tiles, 8, ffp), F32)
    col_chunk = ffp // 2
    assert col_chunk % LANES == 0
    return pl.pallas_call(
        functools.partial(_ffn_kernel, seq_len=t, col_chunk=col_chunk, emit_all=emit_all),
        grid=(n_tiles,),
        in_specs=[pl.BlockSpec((tm, d), lambda i: (i, 0)), _const_spec(g_pre.shape), sc_s, sh_s, gt_s, s0_s, s1_s]
        + [_const_spec(c.shape) for c in consts],
        out_specs=[pl.BlockSpec((tm, d), lambda i: (i, 0)), a_spec],
        out_shape=[jax.ShapeDtypeStruct((m, d), F32), a_shape],
        scratch_shapes=[pltpu.VMEM((8, ffp), F32)],
        compiler_params=_cparams("arbitrary"),
        name="ffn",
    )(x, g_pre, sc_a, sh_a, gt_a, s0_a, s1_a, *consts)


def _decode_kernel(pt_ref, qbd_ref, qlat_ref, qrope_ref, knew_ref, vnew_ref, ckvnew_ref, krnew_ref, lfnew_ref,
                   kt_hbm, vt_hbm, lf_hbm, ckv_hbm, krt_hbm,
                   of_ref, om_ref,
                   kbuf, vbuf, lfbuf, ckvbuf, krbuf, sems, m_sc, l_sc, accf, accm, carry, pad_a, pad_c,
                   *, layer, n_pages, n_chunks, n_new, n_fox_heads, kr_rows):
    cp = n_pages // n_chunks
    page = kt_hbm.shape[-1]
    g = pl.program_id(0)
    total = pl.num_programs(0)
    slot = g % 2
    c = g % n_chunks
    nq = qbd_ref.shape[0]

    def copies(gg, sl):
        seq = gg // n_chunks
        first = seq * n_pages + (n_chunks - 1 - gg % n_chunks) * cp
        out = []
        for j in range(cp):
            pg = pt_ref[first + j]
            lanes = pl.ds(j * page, page)
            out.append(pltpu.make_async_copy(kt_hbm.at[layer, pg], kbuf.at[sl, :, lanes], sems.at[sl, 0]))
            out.append(pltpu.make_async_copy(vt_hbm.at[layer, pg], vbuf.at[sl, :, lanes], sems.at[sl, 1]))
            out.append(pltpu.make_async_copy(lf_hbm.at[layer, pg], lfbuf.at[sl, pl.ds(j * n_fox_heads, n_fox_heads), :],
                                             sems.at[sl, 2]))
            out.append(pltpu.make_async_copy(ckv_hbm.at[layer, pg], ckvbuf.at[sl, pl.ds(j * page, page), :], sems.at[sl, 3]))
            out.append(pltpu.make_async_copy(krt_hbm.at[layer, pg], krbuf.at[sl, 0:kr_rows, lanes], sems.at[sl, 4]))
        return out

    @pl.when(g == 0)
    def _():
        krbuf[...] = jnp.zeros_like(krbuf)
        for cpy in copies(0, 0):
            cpy.start()

    @pl.when(g + 1 < total)
    def _():
        for cpy in copies(g + 1, 1 - slot):
            cpy.start()

    qbd = qbd_ref[...]
    qlat = qlat_ref[...]
    qrope = qrope_ref[...]
    rows = 2 * nq
    ri = lax.broadcasted_iota(jnp.int32, (page, page), 0)
    ci = lax.broadcasted_iota(jnp.int32, (page, page), 1)

    @pl.when(c == 0)
    def _():
        def padded(ref, buf):
            buf[...] = jnp.zeros_like(buf)
            buf[0:n_new, :] = ref[...]
            return buf[...].astype(BF16)

        knew = padded(knew_ref, pad_a)
        s_f = _dot_nt(qbd, knew)
        vnew = padded(vnew_ref, pad_a)
        krnew = padded(krnew_ref, pad_a)
        ckvnew = padded(ckvnew_ref, pad_c)
        tri = (ri <= ci).astype(BF16)
        lh, lm, ll = _split3(lfnew_ref[...])
        cnew = _dot(lh, tri) + _dot(lm, tri) + _dot(ll, tri)
        s_f = s_f - jnp.concatenate([cnew] * n_new, axis=0)
        s_m = _dot_nt(qlat, ckvnew) + _dot_nt(qrope, krnew)
        s = jnp.concatenate([s_f, s_m], axis=0)
        key = lax.broadcasted_iota(jnp.int32, s.shape, 1)
        tok = (lax.broadcasted_iota(jnp.int32, s.shape, 0) % nq) // (nq // n_new)
        s = jnp.where(key <= tok, s, NEG_INF)
        m0 = jnp.max(s, axis=-1, keepdims=True)
        p = jnp.exp(s - m0)
        m_sc[...] = m0
        l_sc[...] = jnp.sum(p, axis=-1, keepdims=True)
        pb = p.astype(BF16)
        accf[...] = _dot(pb[0:nq], vnew)
        accm[...] = _dot(pb[nq:rows], ckvnew)
        carry[...] = jnp.zeros_like(carry)

    for cpy in copies(g, slot):
        cpy.wait()

    lf = lfbuf[slot]
    upper = (ri > ci).astype(BF16)
    ones = jnp.ones((page, page), BF16)
    same_head = (ri % n_fox_heads) == (ci % n_fox_heads)
    later = (same_head & (ci // n_fox_heads > ri // n_fox_heads)).astype(BF16)
    lh, lm, ll = _split3(lf)
    local = _dot(lh, upper) + _dot(lm, upper) + _dot(ll, upper)
    tot = _dot(lh, ones) + _dot(lm, ones) + _dot(ll, ones)
    th, tmid, tl = _split3(tot)
    r_rows = local + _dot(later, th) + _dot(later, tmid) + _dot(later, tl) + carry[...]
    sh = same_head.astype(BF16)
    carry[...] = carry[...] + _dot(sh, th) + _dot(sh, tmid) + _dot(sh, tl)
    bias = jnp.concatenate([r_rows[j * n_fox_heads:(j + 1) * n_fox_heads, :] for j in range(cp)], axis=1)
    bias = jnp.concatenate([bias] * n_new, axis=0)

    kt = kbuf[slot].astype(BF16)
    s_f = _dot(qbd, kt) + bias
    ckv = ckvbuf[slot].astype(BF16)
    krt = krbuf[slot].astype(BF16)
    s_m = _dot_nt(qlat, ckv) + _dot(qrope, krt)
    s = jnp.concatenate([s_f, s_m], axis=0)
    m_prev = m_sc[...]
    m_new = jnp.maximum(m_prev, jnp.max(s, axis=-1, keepdims=True))
    alpha = jnp.exp(m_prev - m_new)
    p = jnp.exp(s - m_new)
    l_sc[...] = alpha * l_sc[...] + jnp.sum(p, axis=-1, keepdims=True)
    m_sc[...] = m_new
    pb = p.astype(BF16)
    vt = vbuf[slot].astype(BF16)
    accf[...] = alpha[0:nq] * accf[...] + _dot_nt(pb[0:nq], vt)
    accm[...] = alpha[nq:rows] * accm[...] + _dot(pb[nq:rows], ckv)

    @pl.when(c == n_chunks - 1)
    def _():
        l = l_sc[...]
        of_ref[...] = accf[...] / l[0:nq]
        om_ref[...] = (accm[...] / l[nq:rows]).astype(om_ref.dtype)


def _decode(page_table, layer, qbd, qlat, qrope, knew, vnew, ckvnew, krnew, lfnew, kt, vt, lf, ckv, krt, n_fox_heads):
    n_seq, nq, _ = qbd.shape
    n_pages = page_table.shape[1]
    n_new = knew.shape[1]
    page = kt.shape[-1]
    cp = min(CHUNK_PAGES, n_pages)
    assert cp * n_fox_heads == page and n_pages % cp == 0
    n_chunks = n_pages // cp
    ct = cp * page
    lat = ckv.shape[-1]
    kr_rows = krt.shape[2]
    seq = lambda a: pl.BlockSpec((None,) + a.shape[1:], lambda g, pt: (g // n_chunks,) + (0,) * (a.ndim - 1))
    any_spec = pl.BlockSpec(memory_space=pl.ANY)
    vm = [qbd, qlat, qrope, knew, vnew, ckvnew, krnew, lfnew]
    grid_spec = pltpu.PrefetchScalarGridSpec(
        num_scalar_prefetch=1,
        grid=(n_seq * n_chunks,),
        in_specs=[seq(a) for a in vm] + [any_spec] * 5,
        out_specs=[pl.BlockSpec((None, nq, LANES), lambda g, pt: (g // n_chunks, 0, 0)),
                   pl.BlockSpec((None, nq, lat), lambda g, pt: (g // n_chunks, 0, 0))],
        scratch_shapes=[pltpu.VMEM((2, page, ct), F32), pltpu.VMEM((2, page, ct), F32),
                        pltpu.VMEM((2, cp * n_fox_heads, page), F32), pltpu.VMEM((2, ct, lat), F32),
                        pltpu.VMEM((2, LANES, ct), F32), pltpu.SemaphoreType.DMA((2, 5)),
                        pltpu.VMEM((2 * nq, 1), F32), pltpu.VMEM((2 * nq, 1), F32),
                        pltpu.VMEM((nq, LANES), F32), pltpu.VMEM((nq, lat), F32),
                        pltpu.VMEM((page, page), F32), pltpu.VMEM((page, LANES), F32), pltpu.VMEM((page, lat), F32)],
    )
    return pl.pallas_call(
        functools.partial(_decode_kernel, layer=layer, n_pages=n_pages, n_chunks=n_chunks, n_new=n_new,
                          n_fox_heads=n_fox_heads, kr_rows=kr_rows),
        grid_spec=grid_spec,
        out_shape=[jax.ShapeDtypeStruct((n_seq, nq, LANES), F32), jax.ShapeDtypeStruct((n_seq, nq, lat), BF16)],
        compiler_params=_cparams("arbitrary"),
        name="decode",
    )(page_table.reshape(-1), *vm, kt, vt, lf, ckv, krt)


def _pad_cols(w, n):
    return jnp.pad(w, ((0, 0), (0, n - w.shape[1])))


def _rope_tables(pos, half):
    inv = ROPE_THETA ** (-jnp.arange(half, dtype=F32) / half)
    ang = pos.astype(F32)[:, None] * inv[None, :]
    return jnp.cos(ang), jnp.sin(ang)


def _prep_layer(p, l, dims):
    d, fw, kvw, fh, cc, ql, kl, rr, nh, nope, vd, ff, ffp = dims
    bf = lambda a: a.astype(BF16)
    w_in = p["w_in"][l]
    o = np.cumsum([0, fw, kvw, kvw, fh, cc, cc, ql, kl, rr, 3 * d])
    seg = lambda i: w_in[:, o[i]:o[i + 1]]
    wkr = seg(8)
    half = rr // 2
    wkrp = jnp.concatenate([-wkr[:, half:], wkr[:, :half]], axis=1)
    ws = dict(wq=bf(seg(0)), wk=bf(seg(1)), wv=bf(seg(2)), wf=bf(_pad_cols(seg(3), LANES)), wga=bf(seg(4)),
              wgb=bf(seg(5)), wcq=bf(seg(6)), wckv=bf(seg(7)), wkr=bf(_pad_cols(wkr, LANES)),
              wkrp=bf(_pad_cols(wkrp, LANES)), wgl=bf(seg(9)))
    w3 = p["w_uq"][l].reshape(ql, nh, nope + rr)
    w_nope, w_rope = w3[..., :nope], w3[..., nope:]
    w_ropep = jnp.concatenate([-w_rope[..., half:], w_rope[..., :half]], axis=-1)
    z = lambda n: jnp.zeros((ql, nh, n), F32)
    wa = jnp.concatenate([w_nope, w_rope, z(LANES - nope - rr)], axis=-1).reshape(ql, nh * LANES)
    wb = jnp.concatenate([z(nope), w_ropep, z(LANES - nope - rr)], axis=-1).reshape(ql, nh * LANES)
    w_uk = p["w_uk"][l]
    w_uv = p["w_uv"][l]
    wk_slots = jnp.pad(jnp.transpose(w_uk, (1, 0, 2)), ((0, 0), (0, 0), (0, LANES - nope))).reshape(kl, nh * LANES)
    wv_parts = []
    for h in range(nh):
        lo = (h % 2) * vd
        wv_parts.append(jnp.pad(w_uv[h], ((0, 0), (lo, LANES - vd - lo))))
    wv_slots = jnp.concatenate(wv_parts, axis=1)
    uk_bd = jax.scipy.linalg.block_diag(*[jnp.pad(w_uk[h].T, ((0, LANES - nope), (0, 0))) for h in range(nh)])
    uv_bd = jax.scipy.linalg.block_diag(*[w_uv[h] for h in range(nh)])
    pad_ff = lambda a: jnp.pad(a, ((0, 0), (0, ffp - ff)))
    return dict(
        ws=ws, wa=bf(wa), wb=bf(wb), wk_slots=bf(wk_slots), wv_slots=bf(wv_slots), uk_bd=bf(uk_bd), uv_bd=bf(uv_bd),
        w_ada=bf(p["w_ada"][l]), b_ada=p["b_ada"][l][None], g_pre_mix=p["g_pre_mix"][l][None],
        g_post_mix=p["g_post_mix"][l][None], g_pre_ffn=p["g_pre_ffn"][l][None], g_post_ffn=p["g_post_ffn"][l][None],
        b_fox_f=_pad_cols(p["b_fox_f"][l][None], LANES), g_cq=p["g_cq"][l][None], g_ckv=p["g_ckv"][l][None],
        w_fox_out=bf(p["w_fox_out"][l]), w_conv_out=bf(p["w_conv_out"][l]), w_mla_out=bf(p["w_mla_out"][l]),
        w_out=bf(p["w_out"][l]), w_dw=p["w_dw"][l], b_dw=p["b_dw"][l][None], ln_g=p["ln_g"][l][None],
        ln_b=p["ln_b"][l][None], w_gate=bf(pad_ff(p["w_gate"][l])), w_val=bf(pad_ff(p["w_val"][l])),
        w_ffn_dw=jnp.pad(p["w_ffn_dw"][l], ((0, 8 - p["w_ffn_dw"].shape[1]), (0, ffp - ff))),
        b_ffn_dw=pad_ff(p["b_ffn_dw"][l][None]), w_down=bf(jnp.pad(p["w_down"][l], ((0, ffp - ff), (0, 0)))),
    )


def _rope_slot_consts(nh, nope, rr):
    ekr = np.zeros((LANES, nh * LANES), np.float32)
    erope = np.zeros((nh * LANES, nh * LANES), np.float32)
    for h in range(nh):
        ekr[np.arange(rr), h * LANES + nope + np.arange(rr)] = 1.0
        erope[h * LANES + nope + np.arange(rr), h * LANES + np.arange(rr)] = 1.0
    return jnp.asarray(ekr, BF16), jnp.asarray(erope, BF16)


def kernel(x_prompt, x_sample, c_prompt, c_sample, cache_fox_k, cache_fox_v, cache_fox_logf, cache_mla_ckv, cache_mla_krope, state_conv, state_ffn_conv, page_table, w_ada, b_ada, g_pre_mix, g_post_mix, g_pre_ffn, g_post_ffn, w_in, b_fox_f, w_fox_out, w_dw, b_dw, ln_g, ln_b, w_conv_out, g_cq, w_uq, g_ckv, w_uk, w_uv, w_mla_out, w_out, w_gate, w_val, w_ffn_dw, b_ffn_dw, w_down):
    p = dict(w_ada=w_ada, b_ada=b_ada, g_pre_mix=g_pre_mix, g_post_mix=g_post_mix, g_pre_ffn=g_pre_ffn,
             g_post_ffn=g_post_ffn, w_in=w_in, b_fox_f=b_fox_f, w_fox_out=w_fox_out, w_dw=w_dw, b_dw=b_dw, ln_g=ln_g,
             ln_b=ln_b, w_conv_out=w_conv_out, g_cq=g_cq, w_uq=w_uq, g_ckv=g_ckv, w_uk=w_uk, w_uv=w_uv,
             w_mla_out=w_mla_out, w_out=w_out, w_gate=w_gate, w_val=w_val, w_ffn_dw=w_ffn_dw, b_ffn_dw=b_ffn_dw,
             w_down=w_down)
    bp, tp, d = x_prompt.shape
    bs, ts, _ = x_sample.shape
    depth, n_phys, page, n_kv, dh = cache_fox_k.shape
    fh = b_fox_f.shape[-1]
    grp = fh // n_kv
    fw, kvw = fh * dh, n_kv * dh
    cc = w_dw.shape[-1]
    ql, kl, rr = g_cq.shape[-1], g_ckv.shape[-1], cache_mla_krope.shape[-1]
    nh, nope, vd = w_uk.shape[1], w_uk.shape[3], w_uv.shape[3]
    ff = w_gate.shape[-1]
    ffp = -(-ff // (2 * LANES)) * (2 * LANES)
    past = page_table.shape[1] * page
    assert dh == vd == LANES // 2 and nope + rr <= LANES and fh <= LANES and nh % 2 == 0 and fh % 2 == 0
    dims = (d, fw, kvw, fh, cc, ql, kl, rr, nh, nope, vd, ff, ffp)
    fox_scale = dh ** -0.5
    mla_scale = (nope + rr) ** -0.5
    half = rr // 2

    def tables(pos):
        cos, sin = _rope_tables(pos, half)
        n = pos.shape[0]
        z = lambda w: jnp.zeros((n, w), F32)
        cos2, sin2 = jnp.concatenate([cos, cos], 1), jnp.concatenate([sin, sin], 1)
        cos_kr = jnp.concatenate([cos2, z(LANES - rr)], 1)
        sin_kr = jnp.concatenate([sin2, z(LANES - rr)], 1)
        cos_q = jnp.concatenate([jnp.full((n, nope), mla_scale, F32), mla_scale * cos2, z(LANES - nope - rr)], 1)
        sin_q = jnp.concatenate([z(nope), mla_scale * sin2, z(LANES - nope - rr)], 1)
        return cos_kr, sin_kr, cos_q, sin_q

    tab_p = tables(jnp.arange(tp, dtype=jnp.int32))
    tab_s = tables(past + jnp.arange(ts, dtype=jnp.int32))
    ekr, erope = _rope_slot_consts(nh, nope, rr)

    kt_all = jnp.transpose(cache_fox_k, (0, 1, 3, 4, 2)).reshape(depth, n_phys, kvw, page)
    vt_all = jnp.transpose(cache_fox_v, (0, 1, 3, 4, 2)).reshape(depth, n_phys, kvw, page)
    lf_all = jnp.transpose(cache_fox_logf, (0, 1, 3, 2))
    krt_all = jnp.transpose(cache_mla_krope, (0, 1, 3, 2))

    xp = x_prompt.reshape(bp * tp, d)
    xs = x_sample.reshape(bs * ts, d)
    c_all = jnp.concatenate([c_prompt, c_sample], axis=0)
    c_pad = -(-c_all.shape[0] // 8) * 8
    c_all = jnp.pad(c_all, ((0, c_pad - c_all.shape[0]), (0, 0)))
    fox_k_slot = [h // grp for h in range(fh)]
    fox_v_slot = [(h // grp) * 2 + h % 2 for h in range(fh)]
    mla_slot = list(range(nh))
    new_p, new_s = [], []

    for l in range(depth):
        lw = _prep_layer(p, l, dims)
        mod = _ada(c_all, lw["w_ada"], lw["b_ada"])
        mod_p = [mod[:bp, i * d:(i + 1) * d] for i in range(6)]
        mod_s = [mod[bp:bp + bs, i * d:(i + 1) * d] for i in range(6)]

        def mixing(x, n_seq, t, md, tab):
            oq, ok, ov, olf, ou, ocq, ockv, okr, og = _inproj(
                x, n_seq, t, lw["g_pre_mix"], md[1], md[0], tab[0], tab[1], lw["b_fox_f"], lw["g_cq"], lw["g_ckv"],
                lw["ws"], fox_scale, fh)
            qm = _mla_q(ocq, n_seq, t, tab[2], tab[3], lw["wa"], lw["wb"], nh)
            return oq, ok, ov, olf, ou, ockv, okr, og, qm

        def channel(x, n_seq, t, md, oa, uc, oc, og, s0, s1, emit_all):
            x = _merge(oa, uc, oc, og, x, md[2], n_seq, t, lw["ln_g"], lw["ln_b"], lw["g_post_mix"],
                       lw["w_fox_out"], lw["w_conv_out"], lw["w_mla_out"], lw["w_out"])
            return _ffn(x, n_seq, t, lw["g_pre_ffn"], md[4], md[3], md[5], s0, s1, lw["w_gate"], lw["w_val"],
                        lw["w_ffn_dw"], lw["b_ffn_dw"], lw["w_down"], lw["g_post_ffn"], emit_all)

        oq, ok, ov, olf, ou, ockv, okr, og, qm = mixing(xp, bp, tp, mod_p, tab_p)
        qf, kf, vf = _fox_prep(oq, ok, ov, olf, tp, fh, n_kv, dh)
        oa = _flash(qf, kf, vf, bp, tp, fox_k_slot, fox_v_slot)
        km, vm = _mla_kv(ockv, okr, lw["wk_slots"], ekr, lw["wv_slots"])
        oc = _flash(qm, km, vm, bp, tp, mla_slot, mla_slot)
        uc = _dwconv(ou, jnp.zeros((bp, w_dw.shape[1] - 1, cc), F32), lw["w_dw"], lw["b_dw"], bp, tp)
        zero_ff = jnp.zeros((bp, ffp), F32)
        xp, tail = channel(xp, bp, tp, mod_p, oa, uc, oc, og, zero_ff, zero_ff, False)
        tiles = tp // min(ROW_TILE, tp)
        ffn_p = tail.reshape(bp, tiles, 8, ffp)[:, -1, 6:8, :ff]
        new_p.append((ok.reshape(bp, tp, n_kv, dh), ov.reshape(bp, tp, n_kv, dh), olf[:, :fh].reshape(bp, tp, fh),
                      ockv.reshape(bp, tp, kl), okr[:, :rr].reshape(bp, tp, rr),
                      ou.reshape(bp, tp, cc)[:, tp - (w_dw.shape[1] - 1):], ffn_p))

        oq, ok, ov, olf, ou, ockv, okr, og, qm = mixing(xs, bs, ts, mod_s, tab_s)
        q4 = oq.reshape(bs, ts * fh, dh)
        own = (jnp.arange(ts * fh) % fh) // grp
        qbd = jnp.concatenate([jnp.where((own == kv)[None, :, None], q4, 0.0) for kv in range(n_kv)], axis=-1)
        qlat = _mm(qm, lw["uk_bd"], BF16, "q_lat").reshape(bs, ts * nh, kl)
        qrope = _mm(qm, erope, BF16, "q_rope").reshape(bs, ts * nh, LANES)
        lfnew = jnp.pad(jnp.transpose(olf[:, :fh].reshape(bs, ts, fh), (0, 2, 1)), ((0, 0), (0, 0), (0, page - ts)))
        of, olat = _decode(page_table, l, qbd.astype(BF16), qlat, qrope, ok.reshape(bs, ts, kvw),
                           ov.reshape(bs, ts, kvw), ockv.reshape(bs, ts, kl), okr.reshape(bs, ts, LANES), lfnew,
                           kt_all, vt_all, lf_all, cache_mla_ckv, krt_all, fh)
        of5 = of.reshape(bs, ts, n_kv, grp, n_kv, dh)
        oa = jnp.concatenate([of5[:, :, kv, :, kv, :] for kv in range(n_kv)], axis=2).reshape(bs * ts, fw).astype(BF16)
        oc = _mm(olat.reshape(bs * ts, nh * kl), lw["uv_bd"], BF16, "o_v")
        uc = _dwconv(ou, state_conv[l], lw["w_dw"], lw["b_dw"], bs, ts)
        sfc = jnp.pad(state_ffn_conv[l], ((0, 0), (0, 0), (0, ffp - ff)))
        xs, a_all = channel(xs, bs, ts, mod_s, oa, uc, oc, og, sfc[:, 0], sfc[:, 1], True)
        hist_c = jnp.concatenate([state_conv[l], ou.reshape(bs, ts, cc)], axis=1)[:, ts:]
        hist_f = jnp.concatenate([state_ffn_conv[l], a_all[:, :ff].reshape(bs, ts, ff)], axis=1)[:, ts:]
        new_s.append((ok.reshape(bs, ts, n_kv, dh), ov.reshape(bs, ts, n_kv, dh), olf[:, :fh].reshape(bs, ts, fh),
                      ockv.reshape(bs, ts, kl), okr[:, :rr].reshape(bs, ts, rr), hist_c, hist_f))

    stack = lambda xs_, i: jnp.stack([s[i] for s in xs_])
    return ((xp.reshape(bp, tp, d), xs.reshape(bs, ts, d)) + tuple(stack(new_p, i) for i in range(7))
            + tuple(stack(new_s, i) for i in range(7)))
```

```python
import functools

import numpy as np
import jax
import jax.numpy as jnp
from jax import lax
from jax.experimental import pallas as pl
from jax.experimental.pallas import tpu as pltpu

F32 = jnp.float32
BF16 = jnp.bfloat16
NORM_EPS = 1e-6
NEG_INF = -1e30
ROPE_THETA = 10000.0
LANES = 128
VMEM_LIMIT = 56 * 1024 * 1024
ROW_TILE = 256
ATT_TILE = 512
LOG2E = 1.4426950408889634
CHUNK_PAGES = 32


def _cparams(*sem):
    return pltpu.CompilerParams(dimension_semantics=sem, vmem_limit_bytes=VMEM_LIMIT)


def _const_spec(shape):
    nd = len(shape)
    return pl.BlockSpec(shape, lambda *_: (0,) * nd)


def _dot(a, b):
    return jnp.dot(a, b, preferred_element_type=F32)


def _dot_nt(a, b):
    return lax.dot_general(a, b, (((1,), (1,)), ((), ())), preferred_element_type=F32)


def _split3(x):
    hi = x.astype(BF16)
    r = x - hi.astype(F32)
    mid = r.astype(BF16)
    lo = (r - mid.astype(F32)).astype(BF16)
    return hi, mid, lo


def _sigmoid(x):
    return 1.0 / (1.0 + jnp.exp(-x))


def _silu(x):
    return x * _sigmoid(x)


def _log_sigmoid(x):
    return jnp.minimum(x, 0.0) - jnp.log1p(jnp.exp(-jnp.abs(x)))


def _rms(x, g):
    return x * lax.rsqrt(jnp.mean(x * x, axis=-1, keepdims=True) + NORM_EPS) * g


def _group_rows(a2d, n_groups, t, tm):
    d = a2d.shape[-1]
    if t % tm == 0:
        per = t // tm
        return a2d.reshape(n_groups, 1, d), pl.BlockSpec((None, 1, d), lambda i: (i // per, 0, 0))
    assert tm % t == 0
    return jnp.repeat(a2d, t, axis=0), pl.BlockSpec((tm, d), lambda i: (i, 0))


def _pos_rows(tab, n_groups, t, tm):
    w = tab.shape[-1]
    if t % tm == 0:
        per = t // tm
        return tab, pl.BlockSpec((tm, w), lambda i: (i % per, 0))
    assert tm % t == 0
    return jnp.tile(tab, (n_groups, 1)), pl.BlockSpec((tm, w), lambda i: (i, 0))


def _ada_kernel(c_ref, w_ref, b_ref, o_ref):
    o_ref[...] = _dot(_silu(c_ref[...]).astype(BF16), w_ref[...]) + b_ref[...]


def _ada(c_all, w, b):
    m, d = c_all.shape
    n = w.shape[1]
    tn = 1024
    return pl.pallas_call(
        _ada_kernel,
        grid=(n // tn,),
        in_specs=[_const_spec((m, d)), pl.BlockSpec((d, tn), lambda j: (0, j)), pl.BlockSpec((1, tn), lambda j: (0, j))],
        out_specs=pl.BlockSpec((m, tn), lambda j: (0, j)),
        out_shape=jax.ShapeDtypeStruct((m, n), F32),
        compiler_params=_cparams("arbitrary"),
        name="ada",
    )(c_all, w, b)


def _mm_kernel(x_ref, w_ref, o_ref):
    o_ref[...] = _dot(x_ref[...].astype(BF16), w_ref[...]).astype(o_ref.dtype)


def _mm(x, w, out_dtype, name):
    m, k = x.shape
    n = w.shape[1]
    tm = min(ROW_TILE, m)
    tn = min(1024, n)
    return pl.pallas_call(
        _mm_kernel,
        grid=(m // tm, n // tn),
        in_specs=[pl.BlockSpec((tm, k), lambda i, j: (i, 0)), pl.BlockSpec((k, tn), lambda i, j: (0, j))],
        out_specs=pl.BlockSpec((tm, tn), lambda i, j: (i, j)),
        out_shape=jax.ShapeDtypeStruct((m, n), out_dtype),
        compiler_params=_cparams("arbitrary", "arbitrary"),
        name=name,
    )(x, w)


def _inproj_kernel(x_ref, g_ref, sc_ref, sh_ref, cos_ref, sin_ref, bf_ref, gcq_ref, gckv_ref,
                   wq, wk, wv, wf, wga, wgb, wcq, wckv, wkr, wkrp, wgl,
                   oq, ok, ov, olf, ou, ocq, ockv, okr, og, *, q_scale, n_fox_heads):
    h = (_rms(x_ref[...], g_ref[...]) * (1.0 + sc_ref[...]) + sh_ref[...]).astype(BF16)
    oq[...] = _dot(h, wq[...]) * q_scale
    ok[...] = _dot(h, wk[...])
    ov[...] = _dot(h, wv[...])
    lf = _log_sigmoid(_dot(h, wf[...]) + bf_ref[...])
    lane = lax.broadcasted_iota(jnp.int32, lf.shape, 1)
    olf[...] = jnp.where(lane < n_fox_heads, lf, 0.0)
    ou[...] = _dot(h, wga[...]) * _sigmoid(_dot(h, wgb[...]))
    ocq[...] = _rms(_dot(h, wcq[...]), gcq_ref[...]).astype(BF16)
    ockv[...] = _rms(_dot(h, wckv[...]), gckv_ref[...])
    okr[...] = _dot(h, wkr[...]) * cos_ref[...] + _dot(h, wkrp[...]) * sin_ref[...]
    og[...] = _sigmoid(_dot(h, wgl[...]))


def _inproj(x, n_groups, t, g_pre, sc, sh, cos_kr, sin_kr, bf, gcq, gckv, ws, q_scale, n_fox_heads):
    m, d = x.shape
    tm = min(ROW_TILE, m)
    sc_a, sc_s = _group_rows(sc, n_groups, t, tm)
    sh_a, sh_s = _group_rows(sh, n_groups, t, tm)
    cos_a, cos_s = _pos_rows(cos_kr, n_groups, t, tm)
    sin_a, sin_s = _pos_rows(sin_kr, n_groups, t, tm)
    row = lambda w, dt: (pl.BlockSpec((tm, w), lambda i: (i, 0)), jax.ShapeDtypeStruct((m, w), dt))
    outs = [row(ws["wq"].shape[1], F32), row(ws["wk"].shape[1], F32), row(ws["wv"].shape[1], F32), row(LANES, F32),
            row(ws["wga"].shape[1], F32), row(ws["wcq"].shape[1], BF16), row(ws["wckv"].shape[1], F32),
            row(LANES, F32), row(ws["wgl"].shape[1], F32)]
    wnames = ["wq", "wk", "wv", "wf", "wga", "wgb", "wcq", "wckv", "wkr", "wkrp", "wgl"]
    vecs = [g_pre, bf, gcq, gckv]
    in_specs = ([pl.BlockSpec((tm, d), lambda i: (i, 0)), _const_spec(g_pre.shape), sc_s, sh_s, cos_s, sin_s]
                + [_const_spec(v.shape) for v in vecs[1:]] + [_const_spec(ws[n].shape) for n in wnames])
    return pl.pallas_call(
        functools.partial(_inproj_kernel, q_scale=q_scale, n_fox_heads=n_fox_heads),
        grid=(m // tm,),
        in_specs=in_specs,
        out_specs=[o[0] for o in outs],
        out_shape=[o[1] for o in outs],
        compiler_params=_cparams("arbitrary"),
        name="inproj",
    )(x, g_pre, sc_a, sh_a, cos_a, sin_a, bf, gcq, gckv, *[ws[n] for n in wnames])


def _fox_prep_kernel(q_ref, k_ref, v_ref, lf_ref, eq, ech, ecm, ecl, qconst, ek, ekh, ekm, ekl, kconst, ev,
                     oq, ok, ov, carry, *, tiles_per_seq):
    tm = q_ref.shape[0]

    @pl.when(pl.program_id(0) % tiles_per_seq == 0)
    def _():
        carry[...] = jnp.zeros_like(carry)

    r = lax.broadcasted_iota(jnp.int32, (tm, tm), 0)
    c = lax.broadcasted_iota(jnp.int32, (tm, tm), 1)
    tri = (c <= r).astype(BF16)
    l_hi, l_mid, l_lo = _split3(lf_ref[...])
    cum = _dot(tri, l_hi) + _dot(tri, l_mid) + _dot(tri, l_lo) + carry[0:1, :]
    carry[...] = jnp.broadcast_to(cum[tm - 1:tm, :], carry.shape)
    c_hi, c_mid, c_lo = _split3(cum * LOG2E)
    qp = (_dot((q_ref[...] * LOG2E).astype(BF16), eq[...]) + _dot(c_hi, ech[...]) + _dot(c_mid, ecm[...])
          + _dot(c_lo, ecl[...]) + qconst[...])
    kp = (_dot(k_ref[...].astype(BF16), ek[...]) + _dot(c_hi, ekh[...]) + _dot(c_mid, ekm[...])
          + _dot(c_lo, ekl[...]) + kconst[...])
    oq[...] = qp.astype(BF16)
    ok[...] = kp.astype(BF16)
    ov[...] = _dot(v_ref[...].astype(BF16), ev[...]).astype(BF16)


def _fox_prep_consts(n_heads, n_kv, dh):
    grp = n_heads // n_kv
    eq = np.zeros((n_heads * dh, n_heads * LANES), np.float32)
    ec = np.zeros((3, LANES, n_heads * LANES), np.float32)
    qconst = np.zeros((1, n_heads * LANES), np.float32)
    ek = np.zeros((n_kv * dh, n_kv * LANES), np.float32)
    ekc = np.zeros((3, LANES, n_kv * LANES), np.float32)
    kconst = np.zeros((1, n_kv * LANES), np.float32)
    ev = np.zeros((n_kv * dh, n_kv * 2 * LANES), np.float32)
    for h in range(n_heads):
        kv, g = h // grp, h % grp
        eq[h * dh + np.arange(dh), h * LANES + np.arange(dh)] = 1.0
        for j in range(3):
            ec[j, h, h * LANES + dh + j] = 1.0
            qconst[0, h * LANES + dh + 3 + 3 * g + j] = 1.0
            ekc[j, h, kv * LANES + dh + 3 + 3 * g + j] = -1.0
    for kv in range(n_kv):
        ek[kv * dh + np.arange(dh), kv * LANES + np.arange(dh)] = 1.0
        kconst[0, kv * LANES + dh + np.arange(3)] = 1.0
        for side in range(2):
            ev[kv * dh + np.arange(dh), (kv * 2 + side) * LANES + side * dh + np.arange(dh)] = 1.0
    b = lambda a: jnp.asarray(a, BF16)
    return (b(eq), b(ec[0]), b(ec[1]), b(ec[2]), jnp.asarray(qconst), b(ek), b(ekc[0]), b(ekc[1]), b(ekc[2]),
            jnp.asarray(kconst), b(ev))


def _fox_prep(oq, ok, ov, olf, t, n_heads, n_kv, dh):
    m = oq.shape[0]
    tm = min(ROW_TILE, t)
    consts = _fox_prep_consts(n_heads, n_kv, dh)
    row = lambda a: pl.BlockSpec((tm, a.shape[1]), lambda i: (i, 0))
    widths = (n_heads * LANES, n_kv * LANES, n_kv * 2 * LANES)
    return pl.pallas_call(
        functools.partial(_fox_prep_kernel, tiles_per_seq=t // tm),
        grid=(m // tm,),
        in_specs=[row(oq), row(ok), row(ov), row(olf)] + [_const_spec(c.shape) for c in consts],
        out_specs=[pl.BlockSpec((tm, w), lambda i: (i, 0)) for w in widths],
        out_shape=[jax.ShapeDtypeStruct((m, w), BF16) for w in widths],
        scratch_shapes=[pltpu.VMEM((8, LANES), F32)],
        compiler_params=_cparams("arbitrary"),
        name="fox_prep",
    )(oq, ok, ov, olf, *consts)


def _mla_q_kernel(cq_ref, cos_ref, sin_ref, wa, wb, oq, *, n_heads):
    cq = cq_ref[...]
    qa = _dot(cq, wa[...])
    qb = _dot(cq, wb[...])
    cos = cos_ref[...]
    sin = sin_ref[...]
    for h in range(n_heads):
        sl = slice(h * LANES, (h + 1) * LANES)
        oq[:, sl] = (qa[:, sl] * cos + qb[:, sl] * sin).astype(BF16)


def _mla_q(ocq, n_groups, t, cos_q, sin_q, wa, wb, n_heads):
    m, k = ocq.shape
    tm = min(ROW_TILE, m)
    cos_a, cos_s = _pos_rows(cos_q, n_groups, t, tm)
    sin_a, sin_s = _pos_rows(sin_q, n_groups, t, tm)
    n = wa.shape[1]
    return pl.pallas_call(
        functools.partial(_mla_q_kernel, n_heads=n_heads),
        grid=(m // tm,),
        in_specs=[pl.BlockSpec((tm, k), lambda i: (i, 0)), cos_s, sin_s, _const_spec(wa.shape), _const_spec(wb.shape)],
        out_specs=pl.BlockSpec((tm, n), lambda i: (i, 0)),
        out_shape=jax.ShapeDtypeStruct((m, n), BF16),
        compiler_params=_cparams("arbitrary"),
        name="mla_q",
    )(ocq, cos_a, sin_a, wa, wb)


def _mla_kv_kernel(ckv_ref, kr_ref, wk, ekr, wv, ok, ov):
    ckv = ckv_ref[...].astype(BF16)
    ok[...] = (_dot(ckv, wk[...]) + _dot(kr_ref[...].astype(BF16), ekr[...])).astype(BF16)
    ov[...] = _dot(ckv, wv[...]).astype(BF16)


def _mla_kv(ockv, okr, wk, ekr, wv):
    m = ockv.shape[0]
    tm = min(ROW_TILE, m)
    row = lambda a: pl.BlockSpec((tm, a.shape[1]), lambda i: (i, 0))
    return pl.pallas_call(
        _mla_kv_kernel,
        grid=(m // tm,),
        in_specs=[row(ockv), row(okr), _const_spec(wk.shape), _const_spec(ekr.shape), _const_spec(wv.shape)],
        out_specs=[pl.BlockSpec((tm, wk.shape[1]), lambda i: (i, 0)), pl.BlockSpec((tm, wv.shape[1]), lambda i: (i, 0))],
        out_shape=[jax.ShapeDtypeStruct((m, wk.shape[1]), BF16), jax.ShapeDtypeStruct((m, wv.shape[1]), BF16)],
        compiler_params=_cparams("arbitrary"),
        name="mla_kv",
    )(ockv, okr, wk, ekr, wv)


def _flash_kernel(q_ref, k_ref, v_ref, o_ref, s_sc, mx_sc, ls_sc, acc_sc, *, shared_k, blk):
    qi = pl.program_id(2)
    nt = blk // LANES
    row = lax.broadcasted_iota(jnp.int32, (blk, blk), 0)
    col = lax.broadcasted_iota(jnp.int32, (blk, blk), 1)
    causal = col <= row

    def fold(s, op):
        r = s[:, 0:LANES]
        for c in range(1, nt):
            r = op(r, s[:, c * LANES:(c + 1) * LANES])
        return r

    for j in range(2):
        q = q_ref[:, j * LANES:(j + 1) * LANES]
        kc = 0 if shared_k else j

        def logits(ki, q=q, kc=kc):
            k = k_ref[pl.ds(pl.multiple_of(ki * blk, blk), blk), kc * LANES:(kc + 1) * LANES]
            return _dot_nt(q, k)

        mx_sc[...] = jnp.full((blk, LANES), NEG_INF, F32)

        def pass1(ki, carry, logits=logits):
            s = logits(ki)
            s_sc[ki] = s
            mx_sc[...] = jnp.maximum(mx_sc[...], fold(s, jnp.maximum))
            return carry

        lax.fori_loop(0, qi, pass1, 0)
        s = jnp.where(causal, logits(qi), NEG_INF)
        s_sc[qi] = s
        m_row = jnp.max(jnp.maximum(mx_sc[...], fold(s, jnp.maximum)), axis=-1, keepdims=True)
        mx_sc[...] = jnp.broadcast_to(m_row, (blk, LANES))
        ls_sc[j] = jnp.zeros((blk, LANES), F32)
        acc_sc[j] = jnp.zeros((blk, LANES), F32)

        def pass2(ki, carry, j=j):
            s = s_sc[ki]
            mb = mx_sc[...]
            ps = [jnp.exp2(s[:, c * LANES:(c + 1) * LANES] - mb) for c in range(nt)]
            tot = ps[0]
            for c in range(1, nt):
                tot = tot + ps[c]
            ls_sc[j] = ls_sc[j] + tot
            v = v_ref[pl.ds(pl.multiple_of(ki * blk, blk), blk), j * LANES:(j + 1) * LANES]
            acc_sc[j] = acc_sc[j] + _dot(jnp.concatenate(ps, axis=1).astype(BF16), v)
            return carry

        lax.fori_loop(0, qi + 1, pass2, 0)

    out = (acc_sc[0] / jnp.sum(ls_sc[0], axis=-1, keepdims=True)
           + acc_sc[1] / jnp.sum(ls_sc[1], axis=-1, keepdims=True))
    o_ref[...] = out.astype(o_ref.dtype)


def _flash(q, k, v, n_seq, t, n_heads, pairs_per_k):
    m = q.shape[0]
    blk = min(ATT_TILE, t)
    nq = t // blk
    n_pairs = n_heads // 2
    shared_k = pairs_per_k > 0
    if shared_k:
        k_spec = pl.BlockSpec((t, LANES), lambda b, hp, i: (b, hp // pairs_per_k))
        v_spec = pl.BlockSpec((t, 2 * LANES), lambda b, hp, i: (b, hp // pairs_per_k))
    else:
        k_spec = pl.BlockSpec((t, 2 * LANES), lambda b, hp, i: (b, hp))
        v_spec = pl.BlockSpec((t, 2 * LANES), lambda b, hp, i: (b, hp))
    return pl.pallas_call(
        functools.partial(_flash_kernel, shared_k=shared_k, blk=blk),
        grid=(n_seq, n_pairs, nq),
        in_specs=[pl.BlockSpec((blk, 2 * LANES), lambda b, hp, i: (b * nq + i, hp)), k_spec, v_spec],
        out_specs=pl.BlockSpec((blk, LANES), lambda b, hp, i: (b * nq + i, hp)),
        out_shape=jax.ShapeDtypeStruct((m, n_pairs * LANES), BF16),
        scratch_shapes=[pltpu.VMEM((nq, blk, blk), F32), pltpu.VMEM((blk, LANES), F32),
                        pltpu.VMEM((2, blk, LANES), F32), pltpu.VMEM((2, blk, LANES), F32)],
        compiler_params=_cparams("arbitrary", "arbitrary", "arbitrary"),
        name="flash",
    )(q, k, v)


def _dwconv_kernel(u_ref, st_ref, w_ref, b_ref, o_ref, xp, *, n_taps, hist):
    tm = u_ref.shape[0]
    t = pl.program_id(1)

    @pl.when(t == 0)
    def _():
        xp[0:hist, :] = st_ref[...]

    @pl.when(t > 0)
    def _():
        xp[0:hist, :] = xp[tm:tm + hist, :]

    xp[hist:hist + tm, :] = u_ref[...]
    off = hist - (n_taps - 1)
    for c0 in range(0, u_ref.shape[1], LANES):
        cs = slice(c0, c0 + LANES)
        acc = jnp.broadcast_to(b_ref[:, cs], (tm, LANES))
        for j in range(n_taps):
            acc = acc + w_ref[j:j + 1, cs] * xp[off + j:off + j + tm, cs]
        o_ref[:, cs] = acc


def _dwconv(u, state, w, b, n_seq, t):
    m, ch = u.shape
    n_taps = w.shape[0]
    hist = 32
    assert n_taps - 1 <= hist
    tm = min(ROW_TILE, t)
    st = jnp.pad(state, ((0, 0), (hist - (n_taps - 1), 0), (0, 0)))
    wp = jnp.pad(w, ((0, hist - n_taps), (0, 0)))
    out = pl.pallas_call(
        functools.partial(_dwconv_kernel, n_taps=n_taps, hist=hist),
        grid=(n_seq, t // tm),
        in_specs=[pl.BlockSpec((None, tm, ch), lambda s, i: (s, i, 0)),
                  pl.BlockSpec((None, hist, ch), lambda s, i: (s, 0, 0)),
                  _const_spec(wp.shape), _const_spec(b.shape)],
        out_specs=pl.BlockSpec((None, tm, ch), lambda s, i: (s, i, 0)),
        out_shape=jax.ShapeDtypeStruct((n_seq, t, ch), F32),
        scratch_shapes=[pltpu.VMEM((tm + hist, ch), F32)],
        compiler_params=_cparams("arbitrary", "arbitrary"),
        name="dwconv",
    )(u.reshape(n_seq, t, ch), st, wp, b)
    return out.reshape(m, ch)


def _merge_kernel(oa_ref, uc_ref, oc_ref, g_ref, x_ref, gt_ref, lng, lnb, gpost, wfo, wco, wmo, wout, o_ref):
    d = x_ref.shape[1]
    uc = uc_ref[...]
    xc = uc - jnp.mean(uc, axis=-1, keepdims=True)
    y = xc * lax.rsqrt(jnp.mean(xc * xc, axis=-1, keepdims=True) + NORM_EPS) * lng[...] + lnb[...]
    br_a = _dot(oa_ref[...], wfo[...])
    br_b = _dot(_silu(y).astype(BF16), wco[...])
    br_c = _dot(oc_ref[...], wmo[...])
    merged = g_ref[:, 0:d] * br_a + g_ref[:, d:2 * d] * br_b + g_ref[:, 2 * d:3 * d] * br_c
    z = _dot(merged.astype(BF16), wout[...])
    o_ref[...] = x_ref[...] + gt_ref[...] * _rms(z, gpost[...])


def _merge(oa, uc, oc, gates, x, gt, n_groups, t, lng, lnb, gpost, wfo, wco, wmo, wout):
    m, d = x.shape
    tm = min(ROW_TILE, m)
    gt_a, gt_s = _group_rows(gt, n_groups, t, tm)
    row = lambda a: pl.BlockSpec((tm, a.shape[1]), lambda i: (i, 0))
    consts = [lng, lnb, gpost, wfo, wco, wmo, wout]
    return pl.pallas_call(
        _merge_kernel,
        grid=(m // tm,),
        in_specs=[row(oa), row(uc), row(oc), row(gates), row(x), gt_s] + [_const_spec(c.shape) for c in consts],
        out_specs=pl.BlockSpec((tm, d), lambda i: (i, 0)),
        out_shape=jax.ShapeDtypeStruct((m, d), F32),
        compiler_params=_cparams("arbitrary"),
        name="merge",
    )(oa, uc, oc, gates, x, gt_a, *consts)


def _ffn_kernel(x_ref, g_ref, sc_ref, sh_ref, gt_ref, s0_ref, s1_ref, wg, wv, wdw, bdw, wd, gpost,
                o_ref, a_ref, carry, *, seq_len, col_chunk, emit_all):
    tm, _ = x_ref.shape
    ffp = wg.shape[1]
    x = x_ref[...]
    h = (_rms(x, g_ref[...]) * (1.0 + sc_ref[...]) + sh_ref[...]).astype(BF16)

    @pl.when(pl.program_id(0) == 0)
    def _():
        carry[...] = jnp.zeros_like(carry)

    r = lax.broadcasted_iota(jnp.int32, (tm, 1), 0)
    pos = (pl.program_id(0) * tm + r) % seq_len
    y = jnp.zeros(o_ref.shape, F32)
    for c0 in range(0, ffp, col_chunk):
        cs = slice(c0, c0 + col_chunk)
        a = _dot(h, wg[:, cs])
        val = _dot(h, wv[:, cs])
        prev = carry[:, cs]
        a1 = pltpu.roll(a, 1, 0)
        a1 = jnp.where(r == 0, prev[7:8, :], a1)
        a2 = pltpu.roll(a, 2, 0)
        a2 = jnp.where(r == 0, prev[6:7, :], jnp.where(r == 1, prev[7:8, :], a2))
        s0 = s0_ref[:, cs]
        s1 = s1_ref[:, cs]
        a1 = jnp.where(pos >= 1, a1, s1)
        a2 = jnp.where(pos >= 2, a2, jnp.where(pos == 0, s0, s1))
        gu = wdw[0:1, cs] * a2 + wdw[1:2, cs] * a1 + wdw[2:3, cs] * a + bdw[:, cs]
        carry[:, cs] = a[tm - 8:tm, :]
        if emit_all:
            a_ref[:, cs] = a
        else:
            a_ref[:, cs] = a[tm - 8:tm, :]
        y = y + _dot((_silu(gu) * val).astype(BF16), wd[cs, :])
    o_ref[...] = x + gt_ref[...] * _rms(y, gpost[...])


def _ffn(x, n_groups, t, g_pre, sc, sh, gt, s0, s1, wg, wv, wdw, bdw, wd, gpost, emit_all):
    m, d = x.shape
    ffp = wg.shape[1]
    tm = min(ROW_TILE, m)
    n_tiles = m // tm
    sc_a, sc_s = _group_rows(sc, n_groups, t, tm)
    sh_a, sh_s = _group_rows(sh, n_groups, t, tm)
    gt_a, gt_s = _group_rows(gt, n_groups, t, tm)
    s0_a, s0_s = _group_rows(s0, n_groups, t, tm)
    s1_a, s1_s = _group_rows(s1, n_groups, t, tm)
    consts = [wg, wv, wdw, bdw, wd, gpost]
    if emit_all:
        a_spec, a_shape = pl.BlockSpec((tm, ffp), lambda i: (i, 0)), jax.ShapeDtypeStruct((m, ffp), F32)
    else:
        a_spec, a_shape = pl.BlockSpec((None, 8, ffp), lambda i: (i, 0, 0)), jax.ShapeDtypeStruct((n_tiles, 8, ffp), F32)
    col_chunk = ffp // 2
    assert col_chunk % LANES == 0
    return pl.pallas_call(
        functools.partial(_ffn_kernel, seq_len=t, col_chunk=col_chunk, emit_all=emit_all),
        grid=(n_tiles,),
        in_specs=[pl.BlockSpec((tm, d), lambda i: (i, 0)), _const_spec(g_pre.shape), sc_s, sh_s, gt_s, s0_s, s1_s]
        + [_const_spec(c.shape) for c in consts],
        out_specs=[pl.BlockSpec((tm, d), lambda i: (i, 0)), a_spec],
        out_shape=[jax.ShapeDtypeStruct((m, d), F32), a_shape],
        scratch_shapes=[pltpu.VMEM((8, ffp), F32)],
        compiler_params=_cparams("arbitrary"),
        name="ffn",
    )(x, g_pre, sc_a, sh_a, gt_a, s0_a, s1_a, *consts)


def _decode_kernel(pt_ref, qbd_ref, qlat_ref, qrope_ref, knew_ref, vnew_ref, ckvnew_ref, krnew_ref, lfnew_ref,
                   kt_hbm, vt_hbm, lf_hbm, ckv_hbm, krt_hbm,
                   of_ref, om_ref,
                   kbuf, vbuf, lfbuf, ckvbuf, krbuf, sems, m_sc, l_sc, accf, accm, carry, pad_a, pad_c,
                   *, layer, n_pages, n_chunks, n_new, n_fox_heads, kr_rows):
    cp = n_pages // n_chunks
    page = kt_hbm.shape[-1]
    g = pl.program_id(0)
    total = pl.num_programs(0)
    slot = g % 2
    c = g % n_chunks
    nq = qbd_ref.shape[0]

    def copies(gg, sl):
        seq = gg // n_chunks
        first = seq * n_pages + (n_chunks - 1 - gg % n_chunks) * cp
        out = []
        for j in range(cp):
            pg = pt_ref[first + j]
            lanes = pl.ds(j * page, page)
            out.append(pltpu.make_async_copy(kt_hbm.at[layer, pg], kbuf.at[sl, :, lanes], sems.at[sl, 0]))
            out.append(pltpu.make_async_copy(vt_hbm.at[layer, pg], vbuf.at[sl, :, lanes], sems.at[sl, 1]))
            out.append(pltpu.make_async_copy(lf_hbm.at[layer, pg], lfbuf.at[sl, pl.ds(j * n_fox_heads, n_fox_heads), :],
                                             sems.at[sl, 2]))
            out.append(pltpu.make_async_copy(ckv_hbm.at[layer, pg], ckvbuf.at[sl, pl.ds(j * page, page), :], sems.at[sl, 3]))
            out.append(pltpu.make_async_copy(krt_hbm.at[layer, pg], krbuf.at[sl, 0:kr_rows, lanes], sems.at[sl, 4]))
        return out

    @pl.when(g == 0)
    def _():
        krbuf[...] = jnp.zeros_like(krbuf)
        for cpy in copies(0, 0):
            cpy.start()

    @pl.when(g + 1 < total)
    def _():
        for cpy in copies(g + 1, 1 - slot):
            cpy.start()

    qbd = qbd_ref[...]
    qlat = qlat_ref[...]
    qrope = qrope_ref[...]
    rows = 2 * nq
    ri = lax.broadcasted_iota(jnp.int32, (page, page), 0)
    ci = lax.broadcasted_iota(jnp.int32, (page, page), 1)

    @pl.when(c == 0)
    def _():
        def padded(ref, buf):
            buf[...] = jnp.zeros_like(buf)
            buf[0:n_new, :] = ref[...]
            return buf[...].astype(BF16)

        knew = padded(knew_ref, pad_a)
        s_f = _dot_nt(qbd, knew)
        vnew = padded(vnew_ref, pad_a)
        krnew = padded(krnew_ref, pad_a)
        ckvnew = padded(ckvnew_ref, pad_c)
        tri = (ri <= ci).astype(BF16)
        lh, lm, ll = _split3(lfnew_ref[...])
        cnew = _dot(lh, tri) + _dot(lm, tri) + _dot(ll, tri)
        s_f = s_f - jnp.concatenate([cnew] * n_new, axis=0)
        s_m = _dot_nt(qlat, ckvnew) + _dot_nt(qrope, krnew)
        s = jnp.concatenate([s_f, s_m], axis=0)
        key = lax.broadcasted_iota(jnp.int32, s.shape, 1)
        tok = (lax.broadcasted_iota(jnp.int32, s.shape, 0) % nq) // (nq // n_new)
        s = jnp.where(key <= tok, s, NEG_INF)
        m0 = jnp.max(s, axis=-1, keepdims=True)
        p = jnp.exp(s - m0)
        m_sc[...] = m0
        l_sc[...] = jnp.sum(p, axis=-1, keepdims=True)
        pb = p.astype(BF16)
        accf[...] = _dot(pb[0:nq], vnew)
        accm[...] = _dot(pb[nq:rows], ckvnew)
        carry[...] = jnp.zeros_like(carry)

    for cpy in copies(g, slot):
        cpy.wait()

    lf = lfbuf[slot]
    upper = (ri > ci).astype(BF16)
    ones = jnp.ones((page, page), BF16)
    n_rows = cp * n_fox_heads
    rr = lax.broadcasted_iota(jnp.int32, (n_rows, n_rows), 0)
    rc = lax.broadcasted_iota(jnp.int32, (n_rows, n_rows), 1)
    same_head = (rr % n_fox_heads) == (rc % n_fox_heads)
    later = (same_head & (rc // n_fox_heads > rr // n_fox_heads)).astype(BF16)
    lh, lm, ll = _split3(lf)
    local = _dot(lh, upper) + _dot(lm, upper) + _dot(ll, upper)
    tot = _dot(lh, ones) + _dot(lm, ones) + _dot(ll, ones)
    th, tmid, tl = _split3(tot)
    r_rows = local + _dot(later, th) + _dot(later, tmid) + _dot(later, tl) + carry[...]
    sh = same_head.astype(BF16)
    carry[...] = carry[...] + _dot(sh, th) + _dot(sh, tmid) + _dot(sh, tl)
    bias = jnp.concatenate([r_rows[j * n_fox_heads:(j + 1) * n_fox_heads, :] for j in range(cp)], axis=1)
    bias = jnp.concatenate([bias] * n_new, axis=0)

    kt = kbuf[slot].astype(BF16)
    s_f = _dot(qbd, kt) + bias
    ckv = ckvbuf[slot].astype(BF16)
    krt = krbuf[slot].astype(BF16)
    s_m = _dot_nt(qlat, ckv) + _dot(qrope, krt)
    s = jnp.concatenate([s_f, s_m], axis=0)
    m_prev = m_sc[...]
    m_new = jnp.maximum(m_prev, jnp.max(s, axis=-1, keepdims=True))
    alpha = jnp.exp(m_prev - m_new)
    p = jnp.exp(s - m_new)
    l_sc[...] = alpha * l_sc[...] + jnp.sum(p, axis=-1, keepdims=True)
    m_sc[...] = m_new
    pb = p.astype(BF16)
    vt = vbuf[slot].astype(BF16)
    accf[...] = alpha[0:nq] * accf[...] + _dot_nt(pb[0:nq], vt)
    accm[...] = alpha[nq:rows] * accm[...] + _dot(pb[nq:rows], ckv)

    @pl.when(c == n_chunks - 1)
    def _():
        l = l_sc[...]
        of_ref[...] = accf[...] / l[0:nq]
        om_ref[...] = (accm[...] / l[nq:rows]).astype(om_ref.dtype)


def _decode(page_table, layer, qbd, qlat, qrope, knew, vnew, ckvnew, krnew, lfnew, kt, vt, lf, ckv, krt, n_fox_heads):
    n_seq, nq, _ = qbd.shape
    n_pages = page_table.shape[1]
    n_new = knew.shape[1]
    page = kt.shape[-1]
    cp = min(CHUNK_PAGES, n_pages)
    assert n_pages % cp == 0
    n_chunks = n_pages // cp
    ct = cp * page
    lat = ckv.shape[-1]
    kr_rows = krt.shape[2]
    seq = lambda a: pl.BlockSpec((None,) + a.shape[1:], lambda g, pt: (g // n_chunks,) + (0,) * (a.ndim - 1))
    any_spec = pl.BlockSpec(memory_space=pl.ANY)
    vm = [qbd, qlat, qrope, knew, vnew, ckvnew, krnew, lfnew]
    grid_spec = pltpu.PrefetchScalarGridSpec(
        num_scalar_prefetch=1,
        grid=(n_seq * n_chunks,),
        in_specs=[seq(a) for a in vm] + [any_spec] * 5,
        out_specs=[pl.BlockSpec((None, nq, LANES), lambda g, pt: (g // n_chunks, 0, 0)),
                   pl.BlockSpec((None, nq, lat), lambda g, pt: (g // n_chunks, 0, 0))],
        scratch_shapes=[pltpu.VMEM((2, page, ct), F32), pltpu.VMEM((2, page, ct), F32),
                        pltpu.VMEM((2, cp * n_fox_heads, page), F32), pltpu.VMEM((2, ct, lat), F32),
                        pltpu.VMEM((2, LANES, ct), F32), pltpu.SemaphoreType.DMA((2, 5)),
                        pltpu.VMEM((2 * nq, 1), F32), pltpu.VMEM((2 * nq, 1), F32),
                        pltpu.VMEM((nq, LANES), F32), pltpu.VMEM((nq, lat), F32),
                        pltpu.VMEM((cp * n_fox_heads, page), F32), pltpu.VMEM((page, LANES), F32),
                        pltpu.VMEM((page, lat), F32)],
    )
    return pl.pallas_call(
        functools.partial(_decode_kernel, layer=layer, n_pages=n_pages, n_chunks=n_chunks, n_new=n_new,
                          n_fox_heads=n_fox_heads, kr_rows=kr_rows),
        grid_spec=grid_spec,
        out_shape=[jax.ShapeDtypeStruct((n_seq, nq, LANES), F32), jax.ShapeDtypeStruct((n_seq, nq, lat), BF16)],
        compiler_params=_cparams("arbitrary"),
        name="decode",
    )(page_table.reshape(-1), *vm, kt, vt, lf, ckv, krt)


def _pad_cols(w, n):
    return jnp.pad(w, ((0, 0), (0, n - w.shape[1])))


def _rope_tables(pos, half):
    inv = ROPE_THETA ** (-jnp.arange(half, dtype=F32) / half)
    ang = pos.astype(F32)[:, None] * inv[None, :]
    return jnp.cos(ang), jnp.sin(ang)


def _prep_layer(p, l, dims):
    d, fw, kvw, fh, cc, ql, kl, rr, nh, nope, vd, ff, ffp = dims
    bf = lambda a: a.astype(BF16)
    w_in = p["w_in"][l]
    o = np.cumsum([0, fw, kvw, kvw, fh, cc, cc, ql, kl, rr, 3 * d])
    seg = lambda i: w_in[:, o[i]:o[i + 1]]
    wkr = seg(8)
    half = rr // 2
    wkrp = jnp.concatenate([-wkr[:, half:], wkr[:, :half]], axis=1)
    ws = dict(wq=bf(seg(0)), wk=bf(seg(1)), wv=bf(seg(2)), wf=bf(_pad_cols(seg(3), LANES)), wga=bf(seg(4)),
              wgb=bf(seg(5)), wcq=bf(seg(6)), wckv=bf(seg(7)), wkr=bf(_pad_cols(wkr, LANES)),
              wkrp=bf(_pad_cols(wkrp, LANES)), wgl=bf(seg(9)))
    w3 = p["w_uq"][l].reshape(ql, nh, nope + rr)
    w_nope, w_rope = w3[..., :nope], w3[..., nope:]
    w_ropep = jnp.concatenate([-w_rope[..., half:], w_rope[..., :half]], axis=-1)
    z = lambda n: jnp.zeros((ql, nh, n), F32)
    wa = jnp.concatenate([w_nope, w_rope, z(LANES - nope - rr)], axis=-1).reshape(ql, nh * LANES)
    wb = jnp.concatenate([z(nope), w_ropep, z(LANES - nope - rr)], axis=-1).reshape(ql, nh * LANES)
    w_uk = p["w_uk"][l]
    w_uv = p["w_uv"][l]
    wk_slots = jnp.pad(jnp.transpose(w_uk, (1, 0, 2)), ((0, 0), (0, 0), (0, LANES - nope))).reshape(kl, nh * LANES)
    wv_parts = []
    for h in range(nh):
        lo = (h % 2) * vd
        wv_parts.append(jnp.pad(w_uv[h], ((0, 0), (lo, LANES - vd - lo))))
    wv_slots = jnp.concatenate(wv_parts, axis=1)
    uk_bd = jax.scipy.linalg.block_diag(*[jnp.pad(w_uk[h].T, ((0, LANES - nope), (0, 0))) for h in range(nh)])
    uv_bd = jax.scipy.linalg.block_diag(*[w_uv[h] for h in range(nh)])
    pad_ff = lambda a: jnp.pad(a, ((0, 0), (0, ffp - ff)))
    return dict(
        ws=ws, wa=bf(wa), wb=bf(wb), wk_slots=bf(wk_slots), wv_slots=bf(wv_slots), uk_bd=bf(uk_bd), uv_bd=bf(uv_bd),
        w_ada=bf(p["w_ada"][l]), b_ada=p["b_ada"][l][None], g_pre_mix=p["g_pre_mix"][l][None],
        g_post_mix=p["g_post_mix"][l][None], g_pre_ffn=p["g_pre_ffn"][l][None], g_post_ffn=p["g_post_ffn"][l][None],
        b_fox_f=_pad_cols(p["b_fox_f"][l][None], LANES), g_cq=p["g_cq"][l][None], g_ckv=p["g_ckv"][l][None],
        w_fox_out=bf(p["w_fox_out"][l]), w_conv_out=bf(p["w_conv_out"][l]), w_mla_out=bf(p["w_mla_out"][l]),
        w_out=bf(p["w_out"][l]), w_dw=p["w_dw"][l], b_dw=p["b_dw"][l][None], ln_g=p["ln_g"][l][None],
        ln_b=p["ln_b"][l][None], w_gate=bf(pad_ff(p["w_gate"][l])), w_val=bf(pad_ff(p["w_val"][l])),
        w_ffn_dw=jnp.pad(p["w_ffn_dw"][l], ((0, 8 - p["w_ffn_dw"].shape[1]), (0, ffp - ff))),
        b_ffn_dw=pad_ff(p["b_ffn_dw"][l][None]), w_down=bf(jnp.pad(p["w_down"][l], ((0, ffp - ff), (0, 0)))),
    )


def _rope_slot_consts(nh, nope, rr):
    ekr = np.zeros((LANES, nh * LANES), np.float32)
    erope = np.zeros((nh * LANES, nh * LANES), np.float32)
    for h in range(nh):
        ekr[np.arange(rr), h * LANES + nope + np.arange(rr)] = 1.0
        erope[h * LANES + nope + np.arange(rr), h * LANES + np.arange(rr)] = 1.0
    return jnp.asarray(ekr, BF16), jnp.asarray(erope, BF16)


def kernel(x_prompt, x_sample, c_prompt, c_sample, cache_fox_k, cache_fox_v, cache_fox_logf, cache_mla_ckv, cache_mla_krope, state_conv, state_ffn_conv, page_table, w_ada, b_ada, g_pre_mix, g_post_mix, g_pre_ffn, g_post_ffn, w_in, b_fox_f, w_fox_out, w_dw, b_dw, ln_g, ln_b, w_conv_out, g_cq, w_uq, g_ckv, w_uk, w_uv, w_mla_out, w_out, w_gate, w_val, w_ffn_dw, b_ffn_dw, w_down):
    p = dict(w_ada=w_ada, b_ada=b_ada, g_pre_mix=g_pre_mix, g_post_mix=g_post_mix, g_pre_ffn=g_pre_ffn,
             g_post_ffn=g_post_ffn, w_in=w_in, b_fox_f=b_fox_f, w_fox_out=w_fox_out, w_dw=w_dw, b_dw=b_dw, ln_g=ln_g,
             ln_b=ln_b, w_conv_out=w_conv_out, g_cq=g_cq, w_uq=w_uq, g_ckv=g_ckv, w_uk=w_uk, w_uv=w_uv,
             w_mla_out=w_mla_out, w_out=w_out, w_gate=w_gate, w_val=w_val, w_ffn_dw=w_ffn_dw, b_ffn_dw=b_ffn_dw,
             w_down=w_down)
    bp, tp, d = x_prompt.shape
    bs, ts, _ = x_sample.shape
    depth, n_phys, page, n_kv, dh = cache_fox_k.shape
    fh = b_fox_f.shape[-1]
    grp = fh // n_kv
    fw, kvw = fh * dh, n_kv * dh
    cc = w_dw.shape[-1]
    ql, kl, rr = g_cq.shape[-1], g_ckv.shape[-1], cache_mla_krope.shape[-1]
    nh, nope, vd = w_uk.shape[1], w_uk.shape[3], w_uv.shape[3]
    ff = w_gate.shape[-1]
    ffp = -(-ff // (2 * LANES)) * (2 * LANES)
    past = page_table.shape[1] * page
    assert dh == vd == LANES // 2 and nope + rr <= LANES and fh <= LANES and nh % 2 == 0 and fh % 2 == 0
    dims = (d, fw, kvw, fh, cc, ql, kl, rr, nh, nope, vd, ff, ffp)
    fox_scale = dh ** -0.5
    mla_scale = (nope + rr) ** -0.5
    half = rr // 2

    def tables(pos, q_scale):
        cos, sin = _rope_tables(pos, half)
        n = pos.shape[0]
        z = lambda w: jnp.zeros((n, w), F32)
        cos2, sin2 = jnp.concatenate([cos, cos], 1), jnp.concatenate([sin, sin], 1)
        cos_kr = jnp.concatenate([cos2, z(LANES - rr)], 1)
        sin_kr = jnp.concatenate([sin2, z(LANES - rr)], 1)
        cos_q = jnp.concatenate([jnp.full((n, nope), q_scale, F32), q_scale * cos2, z(LANES - nope - rr)], 1)
        sin_q = jnp.concatenate([z(nope), q_scale * sin2, z(LANES - nope - rr)], 1)
        return cos_kr, sin_kr, cos_q, sin_q

    tab_p = tables(jnp.arange(tp, dtype=jnp.int32), mla_scale * LOG2E)
    tab_s = tables(past + jnp.arange(ts, dtype=jnp.int32), mla_scale)
    ekr, erope = _rope_slot_consts(nh, nope, rr)

    kt_all = jnp.transpose(cache_fox_k, (0, 1, 3, 4, 2)).reshape(depth, n_phys, kvw, page)
    vt_all = jnp.transpose(cache_fox_v, (0, 1, 3, 4, 2)).reshape(depth, n_phys, kvw, page)
    lf_all = jnp.transpose(cache_fox_logf, (0, 1, 3, 2))
    krt_all = jnp.transpose(cache_mla_krope, (0, 1, 3, 2))

    xp = x_prompt.reshape(bp * tp, d)
    xs = x_sample.reshape(bs * ts, d)
    c_all = jnp.concatenate([c_prompt, c_sample], axis=0)
    c_pad = -(-c_all.shape[0] // 8) * 8
    c_all = jnp.pad(c_all, ((0, c_pad - c_all.shape[0]), (0, 0)))
    new_p, new_s = [], []

    for l in range(depth):
        lw = _prep_layer(p, l, dims)
        mod = _ada(c_all, lw["w_ada"], lw["b_ada"])
        mod_p = [mod[:bp, i * d:(i + 1) * d] for i in range(6)]
        mod_s = [mod[bp:bp + bs, i * d:(i + 1) * d] for i in range(6)]

        def mixing(x, n_seq, t, md, tab):
            oq, ok, ov, olf, ou, ocq, ockv, okr, og = _inproj(
                x, n_seq, t, lw["g_pre_mix"], md[1], md[0], tab[0], tab[1], lw["b_fox_f"], lw["g_cq"], lw["g_ckv"],
                lw["ws"], fox_scale, fh)
            qm = _mla_q(ocq, n_seq, t, tab[2], tab[3], lw["wa"], lw["wb"], nh)
            return oq, ok, ov, olf, ou, ockv, okr, og, qm

        def channel(x, n_seq, t, md, oa, uc, oc, og, s0, s1, emit_all):
            x = _merge(oa, uc, oc, og, x, md[2], n_seq, t, lw["ln_g"], lw["ln_b"], lw["g_post_mix"],
                       lw["w_fox_out"], lw["w_conv_out"], lw["w_mla_out"], lw["w_out"])
            return _ffn(x, n_seq, t, lw["g_pre_ffn"], md[4], md[3], md[5], s0, s1, lw["w_gate"], lw["w_val"],
                        lw["w_ffn_dw"], lw["b_ffn_dw"], lw["w_down"], lw["g_post_ffn"], emit_all)

        oq, ok, ov, olf, ou, ockv, okr, og, qm = mixing(xp, bp, tp, mod_p, tab_p)
        qf, kf, vf = _fox_prep(oq, ok, ov, olf, tp, fh, n_kv, dh)
        oa = _flash(qf, kf, vf, bp, tp, fh, grp // 2)
        km, vm = _mla_kv(ockv, okr, lw["wk_slots"], ekr, lw["wv_slots"])
        oc = _flash(qm, km, vm, bp, tp, nh, 0)
        uc = _dwconv(ou, jnp.zeros((bp, w_dw.shape[1] - 1, cc), F32), lw["w_dw"], lw["b_dw"], bp, tp)
        zero_ff = jnp.zeros((bp, ffp), F32)
        xp, tail = channel(xp, bp, tp, mod_p, oa, uc, oc, og, zero_ff, zero_ff, False)
        tiles = tp // min(ROW_TILE, tp)
        ffn_p = tail.reshape(bp, tiles, 8, ffp)[:, -1, 6:8, :ff]
        new_p.append((ok.reshape(bp, tp, n_kv, dh), ov.reshape(bp, tp, n_kv, dh), olf[:, :fh].reshape(bp, tp, fh),
                      ockv.reshape(bp, tp, kl), okr[:, :rr].reshape(bp, tp, rr),
                      ou.reshape(bp, tp, cc)[:, tp - (w_dw.shape[1] - 1):], ffn_p))

        oq, ok, ov, olf, ou, ockv, okr, og, qm = mixing(xs, bs, ts, mod_s, tab_s)
        q4 = oq.reshape(bs, ts * fh, dh)
        own = (jnp.arange(ts * fh) % fh) // grp
        qbd = jnp.concatenate([jnp.where((own == kv)[None, :, None], q4, 0.0) for kv in range(n_kv)], axis=-1)
        qlat = _mm(qm, lw["uk_bd"], BF16, "q_lat").reshape(bs, ts * nh, kl)
        qrope = _mm(qm, erope, BF16, "q_rope").reshape(bs, ts * nh, LANES)
        lfnew = jnp.pad(jnp.transpose(olf[:, :fh].reshape(bs, ts, fh), (0, 2, 1)), ((0, 0), (0, 0), (0, page - ts)))
        of, olat = _decode(page_table, l, qbd.astype(BF16), qlat, qrope, ok.reshape(bs, ts, kvw),
                           ov.reshape(bs, ts, kvw), ockv.reshape(bs, ts, kl), okr.reshape(bs, ts, LANES), lfnew,
                           kt_all, vt_all, lf_all, cache_mla_ckv, krt_all, fh)
        of5 = of.reshape(bs, ts, n_kv, grp, n_kv, dh)
        oa = jnp.concatenate([of5[:, :, kv, :, kv, :] for kv in range(n_kv)], axis=2).reshape(bs * ts, fw).astype(BF16)
        oc = _mm(olat.reshape(bs * ts, nh * kl), lw["uv_bd"], BF16, "o_v")
        uc = _dwconv(ou, state_conv[l], lw["w_dw"], lw["b_dw"], bs, ts)
        sfc = jnp.pad(state_ffn_conv[l], ((0, 0), (0, 0), (0, ffp - ff)))
        xs, a_all = channel(xs, bs, ts, mod_s, oa, uc, oc, og, sfc[:, 0], sfc[:, 1], True)
        hist_c = jnp.concatenate([state_conv[l], ou.reshape(bs, ts, cc)], axis=1)[:, ts:]
        hist_f = jnp.concatenate([state_ffn_conv[l], a_all[:, :ff].reshape(bs, ts, ff)], axis=1)[:, ts:]
        new_s.append((ok.reshape(bs, ts, n_kv, dh), ov.reshape(bs, ts, n_kv, dh), olf[:, :fh].reshape(bs, ts, fh),
                      ockv.reshape(bs, ts, kl), okr[:, :rr].reshape(bs, ts, rr), hist_c, hist_f))

    stack = lambda xs_, i: jnp.stack([s[i] for s in xs_])
    return ((xp.reshape(bp, tp, d), xs.reshape(bs, ts, d)) + tuple(stack(new_p, i) for i in range(7))
            + tuple(stack(new_s, i) for i in range(7)))
```

```python
import functools

import numpy as np
import jax
import jax.numpy as jnp
from jax import lax
from jax.experimental import pallas as pl
from jax.experimental.pallas import tpu as pltpu

F32 = jnp.float32
BF16 = jnp.bfloat16
NORM_EPS = 1e-6
NEG_INF = -1e30
ROPE_THETA = 10000.0
LANES = 128
VMEM_LIMIT = 56 * 1024 * 1024
ROW_TILE = 512
ATT_TILE = 512
LOG2E = 1.4426950408889634
FLASH_HEADS = 4
CHUNK_PAGES = 32


def _cparams(*sem):
    return pltpu.CompilerParams(dimension_semantics=sem, vmem_limit_bytes=VMEM_LIMIT)


def _const_spec(shape):
    nd = len(shape)
    return pl.BlockSpec(shape, lambda *_: (0,) * nd, pipeline_mode=pl.Buffered(1))


def _dot(a, b):
    return jnp.dot(a, b, preferred_element_type=F32)


def _dot_nt(a, b):
    return lax.dot_general(a, b, (((1,), (1,)), ((), ())), preferred_element_type=F32)


def _split3(x):
    hi = x.astype(BF16)
    r = x - hi.astype(F32)
    mid = r.astype(BF16)
    lo = (r - mid.astype(F32)).astype(BF16)
    return hi, mid, lo


def _sigmoid(x):
    return 1.0 / (1.0 + jnp.exp(-x))


def _silu(x):
    return x * _sigmoid(x)


def _log_sigmoid(x):
    return jnp.minimum(x, 0.0) - jnp.log1p(jnp.exp(-jnp.abs(x)))


def _rms(x, g):
    return x * lax.rsqrt(jnp.mean(x * x, axis=-1, keepdims=True) + NORM_EPS) * g


def _group_rows(a2d, n_groups, t, tm):
    d = a2d.shape[-1]
    if t % tm == 0:
        per = t // tm
        return a2d.reshape(n_groups, 1, d), pl.BlockSpec((None, 1, d), lambda i: (i // per, 0, 0))
    assert tm % t == 0
    return jnp.repeat(a2d, t, axis=0), pl.BlockSpec((tm, d), lambda i: (i, 0))


def _pos_rows(tab, n_groups, t, tm):
    w = tab.shape[-1]
    if t % tm == 0:
        per = t // tm
        return tab, pl.BlockSpec((tm, w), lambda i: (i % per, 0))
    assert tm % t == 0
    return jnp.tile(tab, (n_groups, 1)), pl.BlockSpec((tm, w), lambda i: (i, 0))


def _ada_kernel(c_ref, w_ref, b_ref, o_ref):
    o_ref[...] = _dot(_silu(c_ref[...]).astype(BF16), w_ref[...]) + b_ref[...]


def _ada(c_all, w, b):
    m, d = c_all.shape
    n = w.shape[1]
    tn = 1024
    return pl.pallas_call(
        _ada_kernel,
        grid=(n // tn,),
        in_specs=[_const_spec((m, d)), pl.BlockSpec((d, tn), lambda j: (0, j)), pl.BlockSpec((1, tn), lambda j: (0, j))],
        out_specs=pl.BlockSpec((m, tn), lambda j: (0, j)),
        out_shape=jax.ShapeDtypeStruct((m, n), F32),
        compiler_params=_cparams("arbitrary"),
        name="ada",
    )(c_all, w, b)


def _mm_kernel(x_ref, w_ref, o_ref):
    o_ref[...] = _dot(x_ref[...].astype(BF16), w_ref[...]).astype(o_ref.dtype)


def _mm(x, w, out_dtype, name):
    m, k = x.shape
    n = w.shape[1]
    tm = min(ROW_TILE, m)
    tn = min(1024, n)
    return pl.pallas_call(
        _mm_kernel,
        grid=(m // tm, n // tn),
        in_specs=[pl.BlockSpec((tm, k), lambda i, j: (i, 0)), pl.BlockSpec((k, tn), lambda i, j: (0, j))],
        out_specs=pl.BlockSpec((tm, tn), lambda i, j: (i, j)),
        out_shape=jax.ShapeDtypeStruct((m, n), out_dtype),
        compiler_params=_cparams("arbitrary", "arbitrary"),
        name=name,
    )(x, w)


def _inproj_kernel(x_ref, g_ref, sc_ref, sh_ref, cos_ref, sin_ref, bf_ref, gcq_ref, gckv_ref,
                   wq, wk, wv, wf, wga, wgb, wcq, wckv, wkr, wkrp, wgl,
                   oq, ok, ov, olf, ou, ocq, ockv, okr, og, *, q_scale, n_fox_heads):
    h = (_rms(x_ref[...], g_ref[...]) * (1.0 + sc_ref[...]) + sh_ref[...]).astype(BF16)
    oq[...] = _dot(h, wq[...]) * q_scale
    ok[...] = _dot(h, wk[...])
    ov[...] = _dot(h, wv[...])
    lf = _log_sigmoid(_dot(h, wf[...]) + bf_ref[...])
    lane = lax.broadcasted_iota(jnp.int32, lf.shape, 1)
    olf[...] = jnp.where(lane < n_fox_heads, lf, 0.0)
    ou[...] = _dot(h, wga[...]) * _sigmoid(_dot(h, wgb[...]))
    ocq[...] = _rms(_dot(h, wcq[...]), gcq_ref[...]).astype(BF16)
    ockv[...] = _rms(_dot(h, wckv[...]), gckv_ref[...])
    okr[...] = _dot(h, wkr[...]) * cos_ref[...] + _dot(h, wkrp[...]) * sin_ref[...]
    og[...] = _sigmoid(_dot(h, wgl[...]))


def _inproj(x, n_groups, t, g_pre, sc, sh, cos_kr, sin_kr, bf, gcq, gckv, ws, q_scale, n_fox_heads):
    m, d = x.shape
    tm = min(ROW_TILE, m)
    sc_a, sc_s = _group_rows(sc, n_groups, t, tm)
    sh_a, sh_s = _group_rows(sh, n_groups, t, tm)
    cos_a, cos_s = _pos_rows(cos_kr, n_groups, t, tm)
    sin_a, sin_s = _pos_rows(sin_kr, n_groups, t, tm)
    row = lambda w, dt: (pl.BlockSpec((tm, w), lambda i: (i, 0)), jax.ShapeDtypeStruct((m, w), dt))
    outs = [row(ws["wq"].shape[1], F32), row(ws["wk"].shape[1], F32), row(ws["wv"].shape[1], F32), row(LANES, F32),
            row(ws["wga"].shape[1], F32), row(ws["wcq"].shape[1], BF16), row(ws["wckv"].shape[1], F32),
            row(LANES, F32), row(ws["wgl"].shape[1], F32)]
    wnames = ["wq", "wk", "wv", "wf", "wga", "wgb", "wcq", "wckv", "wkr", "wkrp", "wgl"]
    vecs = [g_pre, bf, gcq, gckv]
    in_specs = ([pl.BlockSpec((tm, d), lambda i: (i, 0)), _const_spec(g_pre.shape), sc_s, sh_s, cos_s, sin_s]
                + [_const_spec(v.shape) for v in vecs[1:]] + [_const_spec(ws[n].shape) for n in wnames])
    return pl.pallas_call(
        functools.partial(_inproj_kernel, q_scale=q_scale, n_fox_heads=n_fox_heads),
        grid=(m // tm,),
        in_specs=in_specs,
        out_specs=[o[0] for o in outs],
        out_shape=[o[1] for o in outs],
        compiler_params=_cparams("arbitrary"),
        name="inproj",
    )(x, g_pre, sc_a, sh_a, cos_a, sin_a, bf, gcq, gckv, *[ws[n] for n in wnames])


def _fox_prep_kernel(q_ref, k_ref, v_ref, lf_ref, eq, ech, ecm, ecl, qconst, ek, ekh, ekm, ekl, kconst, ev,
                     oq, ok, ov, carry, *, tiles_per_seq):
    tm = q_ref.shape[0]

    @pl.when(pl.program_id(0) % tiles_per_seq == 0)
    def _():
        carry[...] = jnp.zeros_like(carry)

    r = lax.broadcasted_iota(jnp.int32, (tm, tm), 0)
    c = lax.broadcasted_iota(jnp.int32, (tm, tm), 1)
    tri = (c <= r).astype(BF16)
    l_hi, l_mid, l_lo = _split3(lf_ref[...])
    cum = _dot(tri, l_hi) + _dot(tri, l_mid) + _dot(tri, l_lo) + carry[0:1, :]
    carry[...] = jnp.broadcast_to(cum[tm - 1:tm, :], carry.shape)
    c_hi, c_mid, c_lo = _split3(cum * LOG2E)
    qp = (_dot((q_ref[...] * LOG2E).astype(BF16), eq[...]) + _dot(c_hi, ech[...]) + _dot(c_mid, ecm[...])
          + _dot(c_lo, ecl[...]) + qconst[...])
    kp = (_dot(k_ref[...].astype(BF16), ek[...]) + _dot(c_hi, ekh[...]) + _dot(c_mid, ekm[...])
          + _dot(c_lo, ekl[...]) + kconst[...])
    oq[...] = qp.astype(BF16)
    ok[...] = kp.astype(BF16)
    ov[...] = _dot(v_ref[...].astype(BF16), ev[...]).astype(BF16)


def _fox_prep_consts(n_heads, n_kv, dh):
    grp = n_heads // n_kv
    eq = np.zeros((n_heads * dh, n_heads * LANES), np.float32)
    ec = np.zeros((3, LANES, n_heads * LANES), np.float32)
    qconst = np.zeros((1, n_heads * LANES), np.float32)
    ek = np.zeros((n_kv * dh, n_kv * LANES), np.float32)
    ekc = np.zeros((3, LANES, n_kv * LANES), np.float32)
    kconst = np.zeros((1, n_kv * LANES), np.float32)
    ev = np.zeros((n_kv * dh, n_kv * 2 * LANES), np.float32)
    for h in range(n_heads):
        kv, g = h // grp, h % grp
        eq[h * dh + np.arange(dh), h * LANES + np.arange(dh)] = 1.0
        for j in range(3):
            ec[j, h, h * LANES + dh + j] = 1.0
            qconst[0, h * LANES + dh + 3 + 3 * g + j] = 1.0
            ekc[j, h, kv * LANES + dh + 3 + 3 * g + j] = -1.0
    for kv in range(n_kv):
        ek[kv * dh + np.arange(dh), kv * LANES + np.arange(dh)] = 1.0
        kconst[0, kv * LANES + dh + np.arange(3)] = 1.0
        for side in range(2):
            ev[kv * dh + np.arange(dh), (kv * 2 + side) * LANES + side * dh + np.arange(dh)] = 1.0
    b = lambda a: jnp.asarray(a, BF16)
    return (b(eq), b(ec[0]), b(ec[1]), b(ec[2]), jnp.asarray(qconst), b(ek), b(ekc[0]), b(ekc[1]), b(ekc[2]),
            jnp.asarray(kconst), b(ev))


def _fox_prep(oq, ok, ov, olf, t, n_heads, n_kv, dh):
    m = oq.shape[0]
    tm = min(ROW_TILE, t)
    consts = _fox_prep_consts(n_heads, n_kv, dh)
    row = lambda a: pl.BlockSpec((tm, a.shape[1]), lambda i: (i, 0))
    widths = (n_heads * LANES, n_kv * LANES, n_kv * 2 * LANES)
    return pl.pallas_call(
        functools.partial(_fox_prep_kernel, tiles_per_seq=t // tm),
        grid=(m // tm,),
        in_specs=[row(oq), row(ok), row(ov), row(olf)] + [_const_spec(c.shape) for c in consts],
        out_specs=[pl.BlockSpec((tm, w), lambda i: (i, 0)) for w in widths],
        out_shape=[jax.ShapeDtypeStruct((m, w), BF16) for w in widths],
        scratch_shapes=[pltpu.VMEM((8, LANES), F32)],
        compiler_params=_cparams("arbitrary"),
        name="fox_prep",
    )(oq, ok, ov, olf, *consts)


def _mla_q_kernel(cq_ref, cos_ref, sin_ref, wa, wb, oq, *, n_heads):
    cq = cq_ref[...]
    qa = _dot(cq, wa[...])
    qb = _dot(cq, wb[...])
    cos = cos_ref[...]
    sin = sin_ref[...]
    for h in range(n_heads):
        sl = slice(h * LANES, (h + 1) * LANES)
        oq[:, sl] = (qa[:, sl] * cos + qb[:, sl] * sin).astype(BF16)


def _mla_q(ocq, n_groups, t, cos_q, sin_q, wa, wb, n_heads):
    m, k = ocq.shape
    tm = min(ROW_TILE, m)
    cos_a, cos_s = _pos_rows(cos_q, n_groups, t, tm)
    sin_a, sin_s = _pos_rows(sin_q, n_groups, t, tm)
    n = wa.shape[1]
    return pl.pallas_call(
        functools.partial(_mla_q_kernel, n_heads=n_heads),
        grid=(m // tm,),
        in_specs=[pl.BlockSpec((tm, k), lambda i: (i, 0)), cos_s, sin_s, _const_spec(wa.shape), _const_spec(wb.shape)],
        out_specs=pl.BlockSpec((tm, n), lambda i: (i, 0)),
        out_shape=jax.ShapeDtypeStruct((m, n), BF16),
        compiler_params=_cparams("arbitrary"),
        name="mla_q",
    )(ocq, cos_a, sin_a, wa, wb)


def _mla_kv_kernel(ckv_ref, kr_ref, wk, ekr, wv, ok, ov):
    ckv = ckv_ref[...].astype(BF16)
    ok[...] = (_dot(ckv, wk[...]) + _dot(kr_ref[...].astype(BF16), ekr[...])).astype(BF16)
    ov[...] = _dot(ckv, wv[...]).astype(BF16)


def _mla_kv(ockv, okr, wk, ekr, wv):
    m = ockv.shape[0]
    tm = min(ROW_TILE, m)
    row = lambda a: pl.BlockSpec((tm, a.shape[1]), lambda i: (i, 0))
    return pl.pallas_call(
        _mla_kv_kernel,
        grid=(m // tm,),
        in_specs=[row(ockv), row(okr), _const_spec(wk.shape), _const_spec(ekr.shape), _const_spec(wv.shape)],
        out_specs=[pl.BlockSpec((tm, wk.shape[1]), lambda i: (i, 0)), pl.BlockSpec((tm, wv.shape[1]), lambda i: (i, 0))],
        out_shape=[jax.ShapeDtypeStruct((m, wk.shape[1]), BF16), jax.ShapeDtypeStruct((m, wv.shape[1]), BF16)],
        compiler_params=_cparams("arbitrary"),
        name="mla_kv",
    )(ockv, okr, wk, ekr, wv)


def _flash_kernel(q_ref, k_ref, v_ref, o_ref, s_sc, mx_sc, mb_sc, ls_sc, acc_sc, *, shared_k, blk, n_hd):
    qi = pl.program_id(2)
    nt = blk // LANES
    row = lax.broadcasted_iota(jnp.int32, (blk, blk), 0)
    col = lax.broadcasted_iota(jnp.int32, (blk, blk), 1)
    causal = col <= row

    def fold(s, op):
        r = s[:, 0:LANES]
        for c in range(1, nt):
            r = op(r, s[:, c * LANES:(c + 1) * LANES])
        return r

    def rows_of(ki):
        return pl.ds(pl.multiple_of(ki * blk, blk), blk)

    def pass1(j, ki, masked):
        kc = 0 if shared_k else j
        s = _dot_nt(q_ref[:, j * LANES:(j + 1) * LANES], k_ref[rows_of(ki), kc * LANES:(kc + 1) * LANES])
        if masked:
            s = jnp.where(causal, s, NEG_INF)
        s_sc[j % 2, ki] = s
        mx_sc[j % 2] = jnp.maximum(mx_sc[j % 2], fold(s, jnp.maximum))

    def pass2(j, ki):
        vc = j % 2 if shared_k else j
        s = s_sc[j % 2, ki]
        mb = mb_sc[j % 2]
        ps = [jnp.exp2(s[:, c * LANES:(c + 1) * LANES] - mb) for c in range(nt)]
        tot = ps[0]
        for c in range(1, nt):
            tot = tot + ps[c]
        ls_sc[j] = ls_sc[j] + tot
        v = v_ref[rows_of(ki), vc * LANES:(vc + 1) * LANES]
        acc_sc[j] = acc_sc[j] + _dot(jnp.concatenate(ps, axis=1).astype(BF16), v)

    def start(j):
        mx_sc[j % 2] = jnp.full((blk, LANES), NEG_INF, F32)
        ls_sc[j] = jnp.zeros((blk, LANES), F32)
        acc_sc[j] = jnp.zeros((blk, LANES), F32)

    def finish_max(j):
        mb_sc[j % 2] = jnp.broadcast_to(jnp.max(mx_sc[j % 2], axis=-1, keepdims=True), (blk, LANES))

    start(0)

    def first(ki, carry):
        pass1(0, ki, False)
        return carry

    lax.fori_loop(0, qi, first, 0)
    pass1(0, qi, True)
    finish_max(0)
    for j in range(1, n_hd):
        start(j)

        def both(ki, carry, j=j):
            pass1(j, ki, False)
            pass2(j - 1, ki)
            return carry

        lax.fori_loop(0, qi, both, 0)
        pass1(j, qi, True)
        pass2(j - 1, qi)
        finish_max(j)

    def last(ki, carry):
        pass2(n_hd - 1, ki)
        return carry

    lax.fori_loop(0, qi + 1, last, 0)
    for hp in range(n_hd // 2):
        a, b = 2 * hp, 2 * hp + 1
        out = (acc_sc[a] / jnp.sum(ls_sc[a], axis=-1, keepdims=True)
               + acc_sc[b] / jnp.sum(ls_sc[b], axis=-1, keepdims=True))
        o_ref[:, hp * LANES:(hp + 1) * LANES] = out.astype(o_ref.dtype)


def _flash(q, k, v, n_seq, t, n_heads, shared_k):
    m = q.shape[0]
    blk = min(ATT_TILE, t)
    nq = t // blk
    n_hd = FLASH_HEADS
    n_grp = n_heads // n_hd
    kw = LANES if shared_k else n_hd * LANES
    vw = 2 * LANES if shared_k else n_hd * LANES
    return pl.pallas_call(
        functools.partial(_flash_kernel, shared_k=shared_k, blk=blk, n_hd=n_hd),
        grid=(n_seq, n_grp, nq),
        in_specs=[pl.BlockSpec((blk, n_hd * LANES), lambda b, hg, i: (b * nq + i, hg)),
                  pl.BlockSpec((t, kw), lambda b, hg, i: (b, hg)),
                  pl.BlockSpec((t, vw), lambda b, hg, i: (b, hg))],
        out_specs=pl.BlockSpec((blk, n_hd // 2 * LANES), lambda b, hg, i: (b * nq + i, hg)),
        out_shape=jax.ShapeDtypeStruct((m, n_heads // 2 * LANES), BF16),
        scratch_shapes=[pltpu.VMEM((2, nq, blk, blk), F32), pltpu.VMEM((2, blk, LANES), F32),
                        pltpu.VMEM((2, blk, LANES), F32), pltpu.VMEM((n_hd, blk, LANES), F32),
                        pltpu.VMEM((n_hd, blk, LANES), F32)],
        compiler_params=_cparams("arbitrary", "arbitrary", "arbitrary"),
        name="flash",
    )(q, k, v)


def _dwconv_kernel(u_ref, st_ref, w_ref, b_ref, o_ref, xp, sh, *, n_taps, hist):
    tm = u_ref.shape[0]
    t = pl.program_id(1)

    @pl.when(t == 0)
    def _():
        xp[0:hist, :] = st_ref[...]

    @pl.when(t > 0)
    def _():
        xp[0:hist, :] = xp[tm:tm + hist, :]

    xp[hist:hist + tm, :] = u_ref[...]
    off = hist - (n_taps - 1)
    sub = 8
    for c0 in range(0, u_ref.shape[1], LANES):
        cs = slice(c0, c0 + LANES)
        acc = jnp.broadcast_to(b_ref[:, cs], (tm, LANES))
        for r in range(sub):
            taps = [j for j in range(n_taps) if (off + j) % sub == r]
            if not taps:
                continue
            span = (off + taps[-1]) // sub * sub + tm
            sh[0:span, :] = xp[r:r + span, cs]
            for j in taps:
                q0 = (off + j) // sub * sub
                acc = acc + w_ref[j:j + 1, cs] * sh[q0:q0 + tm, :]
        o_ref[:, cs] = acc


def _dwconv(u, state, w, b, n_seq, t):
    m, ch = u.shape
    n_taps = w.shape[0]
    hist = 32
    assert n_taps - 1 <= hist
    tm = min(ROW_TILE, t)
    st = jnp.pad(state, ((0, 0), (hist - (n_taps - 1), 0), (0, 0)))
    wp = jnp.pad(w, ((0, hist - n_taps), (0, 0)))
    out = pl.pallas_call(
        functools.partial(_dwconv_kernel, n_taps=n_taps, hist=hist),
        grid=(n_seq, t // tm),
        in_specs=[pl.BlockSpec((None, tm, ch), lambda s, i: (s, i, 0)),
                  pl.BlockSpec((None, hist, ch), lambda s, i: (s, 0, 0)),
                  _const_spec(wp.shape), _const_spec(b.shape)],
        out_specs=pl.BlockSpec((None, tm, ch), lambda s, i: (s, i, 0)),
        out_shape=jax.ShapeDtypeStruct((n_seq, t, ch), F32),
        scratch_shapes=[pltpu.VMEM((tm + hist, ch), F32), pltpu.VMEM((tm + hist, LANES), F32)],
        compiler_params=_cparams("arbitrary", "arbitrary"),
        name="dwconv",
    )(u.reshape(n_seq, t, ch), st, wp, b)
    return out.reshape(m, ch)


def _merge_kernel(oa_ref, uc_ref, oc_ref, g_ref, x_ref, gt_ref, lng, lnb, gpost, wfo, wco, wmo, wout, o_ref):
    d = x_ref.shape[1]
    uc = uc_ref[...]
    xc = uc - jnp.mean(uc, axis=-1, keepdims=True)
    y = xc * lax.rsqrt(jnp.mean(xc * xc, axis=-1, keepdims=True) + NORM_EPS) * lng[...] + lnb[...]
    br_a = _dot(oa_ref[...], wfo[...])
    br_b = _dot(_silu(y).astype(BF16), wco[...])
    br_c = _dot(oc_ref[...], wmo[...])
    merged = g_ref[:, 0:d] * br_a + g_ref[:, d:2 * d] * br_b + g_ref[:, 2 * d:3 * d] * br_c
    z = _dot(merged.astype(BF16), wout[...])
    o_ref[...] = x_ref[...] + gt_ref[...] * _rms(z, gpost[...])


def _merge(oa, uc, oc, gates, x, gt, n_groups, t, lng, lnb, gpost, wfo, wco, wmo, wout):
    m, d = x.shape
    tm = min(ROW_TILE, m)
    gt_a, gt_s = _group_rows(gt, n_groups, t, tm)
    row = lambda a: pl.BlockSpec((tm, a.shape[1]), lambda i: (i, 0))
    consts = [lng, lnb, gpost, wfo, wco, wmo, wout]
    return pl.pallas_call(
        _merge_kernel,
        grid=(m // tm,),
        in_specs=[row(oa), row(uc), row(oc), row(gates), row(x), gt_s] + [_const_spec(c.shape) for c in consts],
        out_specs=pl.BlockSpec((tm, d), lambda i: (i, 0)),
        out_shape=jax.ShapeDtypeStruct((m, d), F32),
        compiler_params=_cparams("arbitrary"),
        name="merge",
    )(oa, uc, oc, gates, x, gt_a, *consts)


def _ffn_kernel(x_ref, g_ref, sc_ref, sh_ref, gt_ref, s0_ref, s1_ref, wg, wv, wdw, bdw, wd, gpost,
                o_ref, a_ref, carry, *, seq_len, col_chunk, emit_all):
    tm, _ = x_ref.shape
    ffp = wg.shape[1]
    x = x_ref[...]
    h = (_rms(x, g_ref[...]) * (1.0 + sc_ref[...]) + sh_ref[...]).astype(BF16)

    @pl.when(pl.program_id(0) == 0)
    def _():
        carry[...] = jnp.zeros_like(carry)

    r = lax.broadcasted_iota(jnp.int32, (tm, 1), 0)
    pos = (pl.program_id(0) * tm + r) % seq_len
    y = jnp.zeros(o_ref.shape, F32)
    for c0 in range(0, ffp, col_chunk):
        cs = slice(c0, c0 + col_chunk)
        a = _dot(h, wg[:, cs])
        val = _dot(h, wv[:, cs])
        prev = carry[:, cs]
        a1 = pltpu.roll(a, 1, 0)
        a1 = jnp.where(r == 0, prev[7:8, :], a1)
        a2 = pltpu.roll(a, 2, 0)
        a2 = jnp.where(r == 0, prev[6:7, :], jnp.where(r == 1, prev[7:8, :], a2))
        s0 = s0_ref[:, cs]
        s1 = s1_ref[:, cs]
        a1 = jnp.where(pos >= 1, a1, s1)
        a2 = jnp.where(pos >= 2, a2, jnp.where(pos == 0, s0, s1))
        gu = wdw[0:1, cs] * a2 + wdw[1:2, cs] * a1 + wdw[2:3, cs] * a + bdw[:, cs]
        carry[:, cs] = a[tm - 8:tm, :]
        if emit_all:
            a_ref[:, cs] = a
        else:
            a_ref[:, cs] = a[tm - 8:tm, :]
        y = y + _dot((_silu(gu) * val).astype(BF16), wd[cs, :])
    o_ref[...] = x + gt_ref[...] * _rms(y, gpost[...])


def _ffn(x, n_groups, t, g_pre, sc, sh, gt, s0, s1, wg, wv, wdw, bdw, wd, gpost, emit_all):
    m, d = x.shape
    ffp = wg.shape[1]
    tm = min(ROW_TILE, m)
    n_tiles = m // tm
    sc_a, sc_s = _group_rows(sc, n_groups, t, tm)
    sh_a, sh_s = _group_rows(sh, n_groups, t, tm)
    gt_a, gt_s = _group_rows(gt, n_groups, t, tm)
    s0_a, s0_s = _group_rows(s0, n_groups, t, tm)
    s1_a, s1_s = _group_rows(s1, n_groups, t, tm)
    consts = [wg, wv, wdw, bdw, wd, gpost]
    if emit_all:
        a_spec, a_shape = pl.BlockSpec((tm, ffp), lambda i: (i, 0)), jax.ShapeDtypeStruct((m, ffp), F32)
    else:
        a_spec, a_shape = pl.BlockSpec((None, 8, ffp), lambda i: (i, 0, 0)), jax.ShapeDtypeStruct((n_tiles, 8, ffp), F32)
    col_chunk = ffp // 2
    assert col_chunk % LANES == 0
    return pl.pallas_call(
        functools.partial(_ffn_kernel, seq_len=t, col_chunk=col_chunk, emit_all=emit_all),
        grid=(n_tiles,),
        in_specs=[pl.BlockSpec((tm, d), lambda i: (i, 0)), _const_spec(g_pre.shape), sc_s, sh_s, gt_s, s0_s, s1_s]
        + [_const_spec(c.shape) for c in consts],
        out_specs=[pl.BlockSpec((tm, d), lambda i: (i, 0)), a_spec],
        out_shape=[jax.ShapeDtypeStruct((m, d), F32), a_shape],
        scratch_shapes=[pltpu.VMEM((8, ffp), F32)],
        compiler_params=_cparams("arbitrary"),
        name="ffn",
    )(x, g_pre, sc_a, sh_a, gt_a, s0_a, s1_a, *consts)


def _decode_kernel(pt_ref, qbd_ref, qlat_ref, qrope_ref, knew_ref, vnew_ref, ckvnew_ref, krnew_ref, lfnew_ref,
                   kt_hbm, vt_hbm, lf_hbm, ckv_hbm, krt_hbm,
                   of_ref, om_ref,
                   kbuf, vbuf, lfbuf, ckvbuf, krbuf, sems, m_sc, l_sc, accf, accm, carry, pad_a, pad_c,
                   *, layer, n_pages, n_chunks, n_new, n_fox_heads, kr_rows):
    cp = n_pages // n_chunks
    page = kt_hbm.shape[-1]
    g = pl.program_id(0)
    total = pl.num_programs(0)
    slot = g % 2
    c = g % n_chunks
    nq = qbd_ref.shape[0]

    def copies(gg, sl):
        seq = gg // n_chunks
        first = seq * n_pages + (n_chunks - 1 - gg % n_chunks) * cp
        out = []
        for j in range(cp):
            pg = pt_ref[first + j]
            lanes = pl.ds(j * page, page)
            out.append(pltpu.make_async_copy(kt_hbm.at[layer, pg], kbuf.at[sl, :, lanes], sems.at[sl, 0]))
            out.append(pltpu.make_async_copy(vt_hbm.at[layer, pg], vbuf.at[sl, :, lanes], sems.at[sl, 1]))
            out.append(pltpu.make_async_copy(lf_hbm.at[layer, pg], lfbuf.at[sl, pl.ds(j * n_fox_heads, n_fox_heads), :],
                                             sems.at[sl, 2]))
            out.append(pltpu.make_async_copy(ckv_hbm.at[layer, pg], ckvbuf.at[sl, pl.ds(j * page, page), :], sems.at[sl, 3]))
            out.append(pltpu.make_async_copy(krt_hbm.at[layer, pg], krbuf.at[sl, :, lanes], sems.at[sl, 4]))
        return out

    @pl.when(g == 0)
    def _():
        for cpy in copies(0, 0):
            cpy.start()

    @pl.when(g + 1 < total)
    def _():
        for cpy in copies(g + 1, 1 - slot):
            cpy.start()

    qbd = qbd_ref[...]
    qlat = qlat_ref[...]
    qrope = qrope_ref[...]
    rows = 2 * nq
    ri = lax.broadcasted_iota(jnp.int32, (page, page), 0)
    ci = lax.broadcasted_iota(jnp.int32, (page, page), 1)

    @pl.when(c == 0)
    def _():
        def padded(ref, buf):
            buf[...] = jnp.zeros_like(buf)
            buf[0:n_new, :] = ref[...]
            return buf[...].astype(BF16)

        knew = padded(knew_ref, pad_a)
        s_f = _dot_nt(qbd, knew)
        vnew = padded(vnew_ref, pad_a)
        krnew = padded(krnew_ref, pad_a)
        ckvnew = padded(ckvnew_ref, pad_c)
        tri = (ri <= ci).astype(BF16)
        lh, lm, ll = _split3(lfnew_ref[...])
        cnew = _dot(lh, tri) + _dot(lm, tri) + _dot(ll, tri)
        s_f = s_f - jnp.concatenate([cnew] * n_new, axis=0)
        s_m = _dot_nt(qlat, ckvnew) + _dot_nt(qrope, krnew)
        s = jnp.concatenate([s_f, s_m], axis=0)
        key = lax.broadcasted_iota(jnp.int32, s.shape, 1)
        tok = (lax.broadcasted_iota(jnp.int32, s.shape, 0) % nq) // (nq // n_new)
        s = jnp.where(key <= tok, s, NEG_INF)
        m0 = jnp.max(s, axis=-1, keepdims=True)
        p = jnp.exp(s - m0)
        m_sc[...] = m0
        l_sc[...] = jnp.sum(p, axis=-1, keepdims=True)
        pb = p.astype(BF16)
        accf[...] = _dot(pb[0:nq], vnew)
        accm[...] = _dot(pb[nq:rows], ckvnew)
        carry[...] = jnp.zeros_like(carry)

    for cpy in copies(g, slot):
        cpy.wait()

    lf = lfbuf[slot]
    upper = (ri > ci).astype(BF16)
    ones = jnp.ones((page, page), BF16)
    n_rows = cp * n_fox_heads
    rr = lax.broadcasted_iota(jnp.int32, (n_rows, n_rows), 0)
    rc = lax.broadcasted_iota(jnp.int32, (n_rows, n_rows), 1)
    same_head = (rr % n_fox_heads) == (rc % n_fox_heads)
    later = (same_head & (rc // n_fox_heads > rr // n_fox_heads)).astype(BF16)
    lh, lm, ll = _split3(lf)
    local = _dot(lh, upper) + _dot(lm, upper) + _dot(ll, upper)
    tot = _dot(lh, ones) + _dot(lm, ones) + _dot(ll, ones)
    th, tmid, tl = _split3(tot)
    r_rows = local + _dot(later, th) + _dot(later, tmid) + _dot(later, tl) + carry[...]
    sh = same_head.astype(BF16)
    carry[...] = carry[...] + _dot(sh, th) + _dot(sh, tmid) + _dot(sh, tl)
    bias = jnp.concatenate([r_rows[j * n_fox_heads:(j + 1) * n_fox_heads, :] for j in range(cp)], axis=1)
    bias = jnp.concatenate([bias] * n_new, axis=0)

    kt = kbuf[slot].astype(BF16)
    s_f = _dot(qbd, kt) + bias
    ckv = ckvbuf[slot].astype(BF16)
    krt = krbuf[slot].astype(BF16)
    s_m = _dot_nt(qlat, ckv) + _dot(qrope[:, 0:kr_rows], krt)
    s = jnp.concatenate([s_f, s_m], axis=0)
    m_prev = m_sc[...]
    m_new = jnp.maximum(m_prev, jnp.max(s, axis=-1, keepdims=True))
    alpha = jnp.exp(m_prev - m_new)
    p = jnp.exp(s - m_new)
    l_sc[...] = alpha * l_sc[...] + jnp.sum(p, axis=-1, keepdims=True)
    m_sc[...] = m_new
    pb = p.astype(BF16)
    vt = vbuf[slot].astype(BF16)
    accf[...] = alpha[0:nq] * accf[...] + _dot_nt(pb[0:nq], vt)
    accm[...] = alpha[nq:rows] * accm[...] + _dot(pb[nq:rows], ckv)

    @pl.when(c == n_chunks - 1)
    def _():
        l = l_sc[...]
        of_ref[...] = accf[...] / l[0:nq]
        om_ref[...] = (accm[...] / l[nq:rows]).astype(om_ref.dtype)


def _decode(page_table, layer, qbd, qlat, qrope, knew, vnew, ckvnew, krnew, lfnew, kt, vt, lf, ckv, krt, n_fox_heads):
    n_seq, nq, _ = qbd.shape
    n_pages = page_table.shape[1]
    n_new = knew.shape[1]
    page = kt.shape[-1]
    cp = min(CHUNK_PAGES, n_pages)
    assert n_pages % cp == 0
    n_chunks = n_pages // cp
    ct = cp * page
    lat = ckv.shape[-1]
    kr_rows = krt.shape[2]
    seq = lambda a: pl.BlockSpec((None,) + a.shape[1:], lambda g, pt: (g // n_chunks,) + (0,) * (a.ndim - 1))
    any_spec = pl.BlockSpec(memory_space=pl.ANY)
    vm = [qbd, qlat, qrope, knew, vnew, ckvnew, krnew, lfnew]
    grid_spec = pltpu.PrefetchScalarGridSpec(
        num_scalar_prefetch=1,
        grid=(n_seq * n_chunks,),
        in_specs=[seq(a) for a in vm] + [any_spec] * 5,
        out_specs=[pl.BlockSpec((None, nq, LANES), lambda g, pt: (g // n_chunks, 0, 0)),
                   pl.BlockSpec((None, nq, lat), lambda g, pt: (g // n_chunks, 0, 0))],
        scratch_shapes=[pltpu.VMEM((2, page, ct), F32), pltpu.VMEM((2, page, ct), F32),
                        pltpu.VMEM((2, cp * n_fox_heads, page), F32), pltpu.VMEM((2, ct, lat), F32),
                        pltpu.VMEM((2, kr_rows, ct), F32), pltpu.SemaphoreType.DMA((2, 5)),
                        pltpu.VMEM((2 * nq, 1), F32), pltpu.VMEM((2 * nq, 1), F32),
                        pltpu.VMEM((nq, LANES), F32), pltpu.VMEM((nq, lat), F32),
                        pltpu.VMEM((cp * n_fox_heads, page), F32), pltpu.VMEM((page, LANES), F32),
                        pltpu.VMEM((page, lat), F32)],
    )
    return pl.pallas_call(
        functools.partial(_decode_kernel, layer=layer, n_pages=n_pages, n_chunks=n_chunks, n_new=n_new,
                          n_fox_heads=n_fox_heads, kr_rows=kr_rows),
        grid_spec=grid_spec,
        out_shape=[jax.ShapeDtypeStruct((n_seq, nq, LANES), F32), jax.ShapeDtypeStruct((n_seq, nq, lat), BF16)],
        compiler_params=_cparams("arbitrary"),
        name="decode",
    )(page_table.reshape(-1), *vm, kt, vt, lf, ckv, krt)


def _pad_cols(w, n):
    return jnp.pad(w, ((0, 0), (0, n - w.shape[1])))


def _rope_tables(pos, half):
    inv = ROPE_THETA ** (-jnp.arange(half, dtype=F32) / half)
    ang = pos.astype(F32)[:, None] * inv[None, :]
    return jnp.cos(ang), jnp.sin(ang)


def _prep_layer(p, l, dims):
    d, fw, kvw, fh, cc, ql, kl, rr, nh, nope, vd, ff, ffp = dims
    bf = lambda a: a.astype(BF16)
    w_in = p["w_in"][l]
    o = np.cumsum([0, fw, kvw, kvw, fh, cc, cc, ql, kl, rr, 3 * d])
    seg = lambda i: w_in[:, o[i]:o[i + 1]]
    wkr = seg(8)
    half = rr // 2
    wkrp = jnp.concatenate([-wkr[:, half:], wkr[:, :half]], axis=1)
    ws = dict(wq=bf(seg(0)), wk=bf(seg(1)), wv=bf(seg(2)), wf=bf(_pad_cols(seg(3), LANES)), wga=bf(seg(4)),
              wgb=bf(seg(5)), wcq=bf(seg(6)), wckv=bf(seg(7)), wkr=bf(_pad_cols(wkr, LANES)),
              wkrp=bf(_pad_cols(wkrp, LANES)), wgl=bf(seg(9)))
    w3 = p["w_uq"][l].reshape(ql, nh, nope + rr)
    w_nope, w_rope = w3[..., :nope], w3[..., nope:]
    w_ropep = jnp.concatenate([-w_rope[..., half:], w_rope[..., :half]], axis=-1)
    z = lambda n: jnp.zeros((ql, nh, n), F32)
    wa = jnp.concatenate([w_nope, w_rope, z(LANES - nope - rr)], axis=-1).reshape(ql, nh * LANES)
    wb = jnp.concatenate([z(nope), w_ropep, z(LANES - nope - rr)], axis=-1).reshape(ql, nh * LANES)
    w_uk = p["w_uk"][l]
    w_uv = p["w_uv"][l]
    wk_slots = jnp.pad(jnp.transpose(w_uk, (1, 0, 2)), ((0, 0), (0, 0), (0, LANES - nope))).reshape(kl, nh * LANES)
    wv_parts = []
    for h in range(nh):
        lo = (h % 2) * vd
        wv_parts.append(jnp.pad(w_uv[h], ((0, 0), (lo, LANES - vd - lo))))
    wv_slots = jnp.concatenate(wv_parts, axis=1)
    uk_bd = jax.scipy.linalg.block_diag(*[jnp.pad(w_uk[h].T, ((0, LANES - nope), (0, 0))) for h in range(nh)])
    uv_bd = jax.scipy.linalg.block_diag(*[w_uv[h] for h in range(nh)])
    pad_ff = lambda a: jnp.pad(a, ((0, 0), (0, ffp - ff)))
    return dict(
        ws=ws, wa=bf(wa), wb=bf(wb), wk_slots=bf(wk_slots), wv_slots=bf(wv_slots), uk_bd=bf(uk_bd), uv_bd=bf(uv_bd),
        w_ada=bf(p["w_ada"][l]), b_ada=p["b_ada"][l][None], g_pre_mix=p["g_pre_mix"][l][None],
        g_post_mix=p["g_post_mix"][l][None], g_pre_ffn=p["g_pre_ffn"][l][None], g_post_ffn=p["g_post_ffn"][l][None],
        b_fox_f=_pad_cols(p["b_fox_f"][l][None], LANES), g_cq=p["g_cq"][l][None], g_ckv=p["g_ckv"][l][None],
        w_fox_out=bf(p["w_fox_out"][l]), w_conv_out=bf(p["w_conv_out"][l]), w_mla_out=bf(p["w_mla_out"][l]),
        w_out=bf(p["w_out"][l]), w_dw=p["w_dw"][l], b_dw=p["b_dw"][l][None], ln_g=p["ln_g"][l][None],
        ln_b=p["ln_b"][l][None], w_gate=bf(pad_ff(p["w_gate"][l])), w_val=bf(pad_ff(p["w_val"][l])),
        w_ffn_dw=jnp.pad(p["w_ffn_dw"][l], ((0, 8 - p["w_ffn_dw"].shape[1]), (0, ffp - ff))),
        b_ffn_dw=pad_ff(p["b_ffn_dw"][l][None]), w_down=bf(jnp.pad(p["w_down"][l], ((0, ffp - ff), (0, 0)))),
    )


def _rope_slot_consts(nh, nope, rr):
    ekr = np.zeros((LANES, nh * LANES), np.float32)
    erope = np.zeros((nh * LANES, nh * LANES), np.float32)
    for h in range(nh):
        ekr[np.arange(rr), h * LANES + nope + np.arange(rr)] = 1.0
        erope[h * LANES + nope + np.arange(rr), h * LANES + np.arange(rr)] = 1.0
    return jnp.asarray(ekr, BF16), jnp.asarray(erope, BF16)


def kernel(x_prompt, x_sample, c_prompt, c_sample, cache_fox_k, cache_fox_v, cache_fox_logf, cache_mla_ckv, cache_mla_krope, state_conv, state_ffn_conv, page_table, w_ada, b_ada, g_pre_mix, g_post_mix, g_pre_ffn, g_post_ffn, w_in, b_fox_f, w_fox_out, w_dw, b_dw, ln_g, ln_b, w_conv_out, g_cq, w_uq, g_ckv, w_uk, w_uv, w_mla_out, w_out, w_gate, w_val, w_ffn_dw, b_ffn_dw, w_down):
    p = dict(w_ada=w_ada, b_ada=b_ada, g_pre_mix=g_pre_mix, g_post_mix=g_post_mix, g_pre_ffn=g_pre_ffn,
             g_post_ffn=g_post_ffn, w_in=w_in, b_fox_f=b_fox_f, w_fox_out=w_fox_out, w_dw=w_dw, b_dw=b_dw, ln_g=ln_g,
             ln_b=ln_b, w_conv_out=w_conv_out, g_cq=g_cq, w_uq=w_uq, g_ckv=g_ckv, w_uk=w_uk, w_uv=w_uv,
             w_mla_out=w_mla_out, w_out=w_out, w_gate=w_gate, w_val=w_val, w_ffn_dw=w_ffn_dw, b_ffn_dw=b_ffn_dw,
             w_down=w_down)
    bp, tp, d = x_prompt.shape
    bs, ts, _ = x_sample.shape
    depth, n_phys, page, n_kv, dh = cache_fox_k.shape
    fh = b_fox_f.shape[-1]
    grp = fh // n_kv
    fw, kvw = fh * dh, n_kv * dh
    cc = w_dw.shape[-1]
    ql, kl, rr = g_cq.shape[-1], g_ckv.shape[-1], cache_mla_krope.shape[-1]
    nh, nope, vd = w_uk.shape[1], w_uk.shape[3], w_uv.shape[3]
    ff = w_gate.shape[-1]
    ffp = -(-ff // (2 * LANES)) * (2 * LANES)
    past = page_table.shape[1] * page
    assert grp == FLASH_HEADS and nh % FLASH_HEADS == 0 and fh == nh
    assert dh == vd == LANES // 2 and nope + rr <= LANES and fh <= LANES
    dims = (d, fw, kvw, fh, cc, ql, kl, rr, nh, nope, vd, ff, ffp)
    fox_scale = dh ** -0.5
    mla_scale = (nope + rr) ** -0.5
    half = rr // 2

    def tables(pos, q_scale):
        cos, sin = _rope_tables(pos, half)
        n = pos.shape[0]
        z = lambda w: jnp.zeros((n, w), F32)
        cos2, sin2 = jnp.concatenate([cos, cos], 1), jnp.concatenate([sin, sin], 1)
        cos_kr = jnp.concatenate([cos2, z(LANES - rr)], 1)
        sin_kr = jnp.concatenate([sin2, z(LANES - rr)], 1)
        cos_q = jnp.concatenate([jnp.full((n, nope), q_scale, F32), q_scale * cos2, z(LANES - nope - rr)], 1)
        sin_q = jnp.concatenate([z(nope), q_scale * sin2, z(LANES - nope - rr)], 1)
        return cos_kr, sin_kr, cos_q, sin_q

    tab_p = tables(jnp.arange(tp, dtype=jnp.int32), mla_scale * LOG2E)
    tab_s = tables(past + jnp.arange(ts, dtype=jnp.int32), mla_scale)
    ekr, erope = _rope_slot_consts(nh, nope, rr)

    kt_all = jnp.transpose(cache_fox_k, (0, 1, 3, 4, 2)).reshape(depth, n_phys, kvw, page)
    vt_all = jnp.transpose(cache_fox_v, (0, 1, 3, 4, 2)).reshape(depth, n_phys, kvw, page)
    lf_all = jnp.transpose(cache_fox_logf, (0, 1, 3, 2))
    krt_all = jnp.transpose(cache_mla_krope, (0, 1, 3, 2))

    xp = x_prompt.reshape(bp * tp, d)
    xs = x_sample.reshape(bs * ts, d)
    c_all = jnp.concatenate([c_prompt, c_sample], axis=0)
    c_pad = -(-c_all.shape[0] // 8) * 8
    c_all = jnp.pad(c_all, ((0, c_pad - c_all.shape[0]), (0, 0)))
    new_p, new_s = [], []

    for l in range(depth):
        lw = _prep_layer(p, l, dims)
        mod = _ada(c_all, lw["w_ada"], lw["b_ada"])
        mod_p = [mod[:bp, i * d:(i + 1) * d] for i in range(6)]
        mod_s = [mod[bp:bp + bs, i * d:(i + 1) * d] for i in range(6)]

        def mixing(x, n_seq, t, md, tab):
            oq, ok, ov, olf, ou, ocq, ockv, okr, og = _inproj(
                x, n_seq, t, lw["g_pre_mix"], md[1], md[0], tab[0], tab[1], lw["b_fox_f"], lw["g_cq"], lw["g_ckv"],
                lw["ws"], fox_scale, fh)
            qm = _mla_q(ocq, n_seq, t, tab[2], tab[3], lw["wa"], lw["wb"], nh)
            return oq, ok, ov, olf, ou, ockv, okr, og, qm

        def channel(x, n_seq, t, md, oa, uc, oc, og, s0, s1, emit_all):
            x = _merge(oa, uc, oc, og, x, md[2], n_seq, t, lw["ln_g"], lw["ln_b"], lw["g_post_mix"],
                       lw["w_fox_out"], lw["w_conv_out"], lw["w_mla_out"], lw["w_out"])
            return _ffn(x, n_seq, t, lw["g_pre_ffn"], md[4], md[3], md[5], s0, s1, lw["w_gate"], lw["w_val"],
                        lw["w_ffn_dw"], lw["b_ffn_dw"], lw["w_down"], lw["g_post_ffn"], emit_all)

        oq, ok, ov, olf, ou, ockv, okr, og, qm = mixing(xp, bp, tp, mod_p, tab_p)
        qf, kf, vf = _fox_prep(oq, ok, ov, olf, tp, fh, n_kv, dh)
        oa = _flash(qf, kf, vf, bp, tp, fh, True)
        km, vm = _mla_kv(ockv, okr, lw["wk_slots"], ekr, lw["wv_slots"])
        oc = _flash(qm, km, vm, bp, tp, nh, False)
        uc = _dwconv(ou, jnp.zeros((bp, w_dw.shape[1] - 1, cc), F32), lw["w_dw"], lw["b_dw"], bp, tp)
        zero_ff = jnp.zeros((bp, ffp), F32)
        xp, tail = channel(xp, bp, tp, mod_p, oa, uc, oc, og, zero_ff, zero_ff, False)
        tiles = tp // min(ROW_TILE, tp)
        ffn_p = tail.reshape(bp, tiles, 8, ffp)[:, -1, 6:8, :ff]
        new_p.append((ok.reshape(bp, tp, n_kv, dh), ov.reshape(bp, tp, n_kv, dh), olf[:, :fh].reshape(bp, tp, fh),
                      ockv.reshape(bp, tp, kl), okr[:, :rr].reshape(bp, tp, rr),
                      ou.reshape(bp, tp, cc)[:, tp - (w_dw.shape[1] - 1):], ffn_p))

        oq, ok, ov, olf, ou, ockv, okr, og, qm = mixing(xs, bs, ts, mod_s, tab_s)
        q4 = oq.reshape(bs, ts * fh, dh)
        own = (jnp.arange(ts * fh) % fh) // grp
        qbd = jnp.concatenate([jnp.where((own == kv)[None, :, None], q4, 0.0) for kv in range(n_kv)], axis=-1)
        qlat = _mm(qm, lw["uk_bd"], BF16, "q_lat").reshape(bs, ts * nh, kl)
        qrope = _mm(qm, erope, BF16, "q_rope").reshape(bs, ts * nh, LANES)
        lfnew = jnp.pad(jnp.transpose(olf[:, :fh].reshape(bs, ts, fh), (0, 2, 1)), ((0, 0), (0, 0), (0, page - ts)))
        of, olat = _decode(page_table, l, qbd.astype(BF16), qlat, qrope, ok.reshape(bs, ts, kvw),
                           ov.reshape(bs, ts, kvw), ockv.reshape(bs, ts, kl), okr.reshape(bs, ts, LANES), lfnew,
                           kt_all, vt_all, lf_all, cache_mla_ckv, krt_all, fh)
        of5 = of.reshape(bs, ts, n_kv, grp, n_kv, dh)
        oa = jnp.concatenate([of5[:, :, kv, :, kv, :] for kv in range(n_kv)], axis=2).reshape(bs * ts, fw).astype(BF16)
        oc = _mm(olat.reshape(bs * ts, nh * kl), lw["uv_bd"], BF16, "o_v")
        uc = _dwconv(ou, state_conv[l], lw["w_dw"], lw["b_dw"], bs, ts)
        sfc = jnp.pad(state_ffn_conv[l], ((0, 0), (0, 0), (0, ffp - ff)))
        xs, a_all = channel(xs, bs, ts, mod_s, oa, uc, oc, og, sfc[:, 0], sfc[:, 1], True)
        hist_c = jnp.concatenate([state_conv[l], ou.reshape(bs, ts, cc)], axis=1)[:, ts:]
        hist_f = jnp.concatenate([state_ffn_conv[l], a_all[:, :ff].reshape(bs, ts, ff)], axis=1)[:, ts:]
        new_s.append((ok.reshape(bs, ts, n_kv, dh), ov.reshape(bs, ts, n_kv, dh), olf[:, :fh].reshape(bs, ts, fh),
                      ockv.reshape(bs, ts, kl), okr[:, :rr].reshape(bs, ts, rr), hist_c, hist_f))

    stack = lambda xs_, i: jnp.stack([s[i] for s in xs_])
    return ((xp.reshape(bp, tp, d), xs.reshape(bs, ts, d)) + tuple(stack(new_p, i) for i in range(7))
            + tuple(stack(new_s, i) for i in range(7)))
```
